```python
import jax
import jax.numpy as jnp
from jax import lax
import numpy as np

D_MODEL = 4096
BATCH = 2
SEQ = 4096
DEPTH = 4

GRID_W = 64
CTX_LEN = 256
N_MIXERS = 3
N_MOD = 6

MLA_HEADS = 32
MLA_Q_LORA = 1024
MLA_KV_LORA = 512
MLA_NOPE = 128
MLA_ROPE = 64
MLA_V = 128
MLA_QK = MLA_NOPE + MLA_ROPE
ROPE_BASE = 10000.0
Q_BLOCK = 128

RET_HEADS = 16
RET_DK = D_MODEL // RET_HEADS
RET_DV = D_MODEL // RET_HEADS
RET_CHUNK = 128
RET_THETA_BASE = 10000.0

CONV_WIDTH = 31

MOE_GROUPS = 4
MOE_PER_GROUP = 4
MOE_EXPERTS = MOE_GROUPS * MOE_PER_GROUP
MOE_TOP_K = 2
MOE_D_FF = 384

NORM_EPS = 1e-6
F32 = jnp.float32

kernel_name = 'hybrid_mla_retention_conformer_hmoe_dit'


def rms_norm(x, g):
    xf = x.astype(F32)
    y = xf * lax.rsqrt(jnp.mean(xf * xf, axis=-1, keepdims=True) + NORM_EPS)
    return (y * g.astype(F32)).astype(x.dtype)


def layer_norm(x, g, b):
    xf = x.astype(F32)
    mu = jnp.mean(xf, axis=-1, keepdims=True)
    var = jnp.mean(jnp.square(xf - mu), axis=-1, keepdims=True)
    y = (xf - mu) * lax.rsqrt(var + NORM_EPS) * g.astype(F32) + b.astype(F32)
    return y.astype(x.dtype)


def head_group_norm(y):
    mu = jnp.mean(y, axis=-1, keepdims=True)
    var = jnp.mean(jnp.square(y - mu), axis=-1, keepdims=True)
    return (y - mu) * lax.rsqrt(var + 1e-5)


def modulate(h, shift, scale):
    return h * (1.0 + scale) + shift


def rotate(x, cos, sin):
    half = x.shape[-1] // 2
    xf = x.astype(F32)
    x1, x2 = xf[..., :half], xf[..., half:]
    return jnp.concatenate([x1 * cos - x2 * sin, x1 * sin + x2 * cos], axis=-1).astype(x.dtype)


def axial_rope_tables(rows, rot_dim):
    grid = jnp.stack(jnp.meshgrid(jnp.arange(rows), jnp.arange(GRID_W), indexing='ij'), axis=-1)
    grid = grid.reshape(-1, 2).astype(F32)
    n_freq = rot_dim // 4
    inv = ROPE_BASE ** (-jnp.arange(n_freq, dtype=F32) / n_freq)
    ang = jnp.concatenate([grid[:, :1] * inv, grid[:, 1:] * inv], axis=-1)
    return jnp.cos(ang), jnp.sin(ang)


def retention_tables(n_tokens):
    theta = 1.0 / (RET_THETA_BASE ** jnp.linspace(0.0, 1.0, RET_DK // 2, dtype=F32))
    ang = jnp.arange(n_tokens, dtype=F32)[:, None] * theta
    return jnp.cos(ang), jnp.sin(ang)


def mla_attend(qn, qr, kn, kr, v):
    s = jnp.einsum('bqhd,bkhd->bhqk', qn, kn) + jnp.einsum('bqhr,bkr->bhqk', qr, kr)
    p = jax.nn.softmax(s.astype(F32) * (MLA_QK ** -0.5), axis=-1).astype(v.dtype)
    return jnp.einsum('bhqk,bkhd->bqhd', p, v)


def mla_mixer(hc, hx, wq_a, q_norm, wq_b, wkv_a, kv_norm, wkv_b, wo, cos, sin, need_ctx):
    def project(h):
        cq = rms_norm(h @ wq_a, q_norm)
        q = jnp.einsum('btr,rhd->bthd', cq, wq_b)
        kv = h @ wkv_a
        ckv = rms_norm(kv[..., :MLA_KV_LORA], kv_norm)
        kr = kv[..., MLA_KV_LORA:]
        kvu = jnp.einsum('btr,rhd->bthd', ckv, wkv_b)
        return q[..., :MLA_NOPE], q[..., MLA_NOPE:], kr, kvu[..., :MLA_NOPE], kvu[..., MLA_NOPE:]

    qn_c, qr_c, kr_c, kn_c, v_c = project(hc)
    qn_x, qr_x, kr_x, kn_x, v_x = project(hx)
    qr_x = rotate(qr_x, cos[None, :, None, :], sin[None, :, None, :])
    kr_x = rotate(kr_x, cos[None], sin[None])
    kn_all = jnp.concatenate([kn_c, kn_x], axis=1)
    kr_all = jnp.concatenate([kr_c, kr_x], axis=1)
    v_all = jnp.concatenate([v_c, v_x], axis=1)
    B, S = hx.shape[0], hx.shape[1]
    nb = S // Q_BLOCK

    def blocks(t):
        return jnp.moveaxis(t.reshape((B, nb, Q_BLOCK) + t.shape[2:]), 1, 0)

    o_blk = lax.map(lambda qb: mla_attend(qb[0], qb[1], kn_all, kr_all, v_all),
                    (blocks(qn_x), blocks(qr_x)))
    o_x = jnp.moveaxis(o_blk, 0, 1).reshape(B, S, MLA_HEADS, MLA_V)
    out_x = jnp.einsum('bthd,hdm->btm', o_x, wo)
    out_c = None
    if need_ctx:
        out_c = jnp.einsum('bthd,hdm->btm', mla_attend(qn_c, qr_c, kn_c, kr_c, v_c), wo)
    return out_c, out_x


def chunk_retention(q, k, v, log_g, state):
    B, T, H = q.shape[0], q.shape[1], q.shape[2]
    n = T // RET_CHUNK

    def chunks(t):
        return jnp.moveaxis(t.reshape(B, n, RET_CHUNK, H, t.shape[-1]), 1, 0)

    idx = jnp.arange(RET_CHUNK, dtype=F32)
    diff = idx[:, None] - idx[None, :]
    lower = diff >= 0
    dmat = jnp.where(lower[None], jnp.exp(jnp.where(lower, diff, 0.0)[None] * log_g[:, None, None]), 0.0)
    q_dec = jnp.exp((idx[:, None] + 1.0) * log_g[None, :])
    k_dec = jnp.exp((RET_CHUNK - 1.0 - idx)[:, None] * log_g[None, :])
    blk_dec = jnp.exp(RET_CHUNK * log_g)

    def step(s, qkv):
        qc, kc, vc = qkv
        scores = jnp.einsum('bihd,bjhd->bhij', qc, kc) * dmat
        inner = jnp.einsum('bhij,bjhe->bihe', scores, vc)
        cross = jnp.einsum('bihd,bhde->bihe', qc * q_dec[None, :, :, None], s)
        s_new = s * blk_dec[None, :, None, None] + jnp.einsum('bjhd,bjhe->bhde', kc * k_dec[None, :, :, None], vc)
        return s_new, inner + cross

    s_fin, ys = lax.scan(step, state, (chunks(q), chunks(k), chunks(v)))
    return jnp.moveaxis(ys, 0, 1).reshape(B, T, H, v.shape[-1]), s_fin


def retention_mixer(hc, hx, wq, wk, wv, wg, wo, decay, cos, sin, need_ctx):
    log_g = -jnp.exp(decay.astype(F32))

    def project(h):
        q = jnp.einsum('btd,dhk->bthk', h, wq).astype(F32)
        k = jnp.einsum('btd,dhk->bthk', h, wk).astype(F32) * (RET_DK ** -0.5)
        v = jnp.einsum('btd,dhe->bthe', h, wv).astype(F32)
        return q, k, v

    def flip(t):
        return t[:, ::-1]

    qc, kc, vc = project(hc)
    qx, kx, vx = project(hx)
    qx = rotate(qx, cos[None, :, None, :], sin[None, :, None, :])
    kx = rotate(kx, cos[None, :, None, :], sin[None, :, None, :])
    zero = jnp.zeros((hx.shape[0], RET_HEADS, RET_DK, RET_DV), F32)
    yc_f, s_f = chunk_retention(qc, kc, vc, log_g[0], zero)
    yc_b, s_b = chunk_retention(flip(qc), flip(kc), flip(vc), log_g[1], zero)
    yx_f, _ = chunk_retention(qx, kx, vx, log_g[0], s_f)
    yx_b, _ = chunk_retention(flip(qx), flip(kx), flip(vx), log_g[1], s_b)

    def finish(y, h):
        y = head_group_norm(y).reshape(h.shape[0], h.shape[1], RET_HEADS * RET_DV).astype(h.dtype)
        return (jax.nn.silu(h @ wg) * y) @ wo

    out_x = finish(yx_f + flip(yx_b), hx)
    out_c = finish(yc_f + flip(yc_b), hc) if need_ctx else None
    return out_c, out_x


def conformer_conv(h, pw1, b1, dw, dw_b, ln_g, ln_b, pw2, b2):
    u = h @ pw1 + b1
    a, gt = jnp.split(u, 2, axis=-1)
    u = a * jax.nn.sigmoid(gt)
    pad = CONV_WIDTH // 2
    u = lax.conv_general_dilated(u, dw[:, None, :], window_strides=(1,), padding=[(pad, pad)],
                                 dimension_numbers=('NWC', 'WIO', 'NWC'),
                                 feature_group_count=u.shape[-1]) + dw_b
    u = layer_norm(u, ln_g, ln_b)
    return jax.nn.silu(u) @ pw2 + b2


def hier_moe(h, wg_r, bg_r, we_r, be_r, w1, w3, w2):
    n_tok = h.shape[0]
    pg = jax.nn.softmax((h @ wg_r + bg_r).astype(F32), axis=-1)
    g_sel = jnp.argmax(pg, axis=-1)
    pg_sel = jnp.max(pg, axis=-1, keepdims=True)
    le = (jnp.einsum('nd,dge->nge', h, we_r) + be_r).astype(F32)
    le_sel = le[jnp.arange(n_tok), g_sel]
    top_p, top_i = lax.top_k(jax.nn.softmax(le_sel, axis=-1), MOE_TOP_K)
    w = pg_sel * top_p / jnp.sum(top_p, axis=-1, keepdims=True)
    eid = g_sel[:, None] * MOE_PER_GROUP + top_i
    gates = jnp.sum(jax.nn.one_hot(eid, MOE_EXPERTS, dtype=F32) * w[..., None], axis=1).astype(h.dtype)
    act = jax.nn.silu(jnp.einsum('nd,edf->nef', h, w1)) * jnp.einsum('nd,edf->nef', h, w3)
    return jnp.einsum('nef,efd->nd', act * gates[..., None], w2)


def setup_inputs(seed: int = 0) -> dict:
    key = jax.random.key(seed)
    keys = iter(jax.random.split(key, 48))
    D = D_MODEL

    def normal(shape, fan_in, scale=1.0):
        return jax.random.normal(next(keys), shape, F32) * (scale * fan_in ** -0.5)

    def gain(shape):
        return 1.0 + 0.02 * jax.random.normal(next(keys), shape, F32)

    def small(shape, scale=0.02):
        return scale * jax.random.normal(next(keys), shape, F32)

    n_mla = len(range(0, DEPTH, N_MIXERS))
    n_ret = len(range(1, DEPTH, N_MIXERS))
    n_conv = len(range(2, DEPTH, N_MIXERS))
    decay_base = jnp.log(-jnp.log1p(-(2.0 ** (-5.0 - jnp.arange(RET_HEADS, dtype=F32)))))
    return {
        'x': jax.random.normal(next(keys), (BATCH, SEQ, D), F32),
        'c': jax.random.normal(next(keys), (BATCH, D), F32),
        'ctx': jax.random.normal(next(keys), (BATCH, CTX_LEN, D), F32),
        'c_ctx': jax.random.normal(next(keys), (D,), F32),
        'ada_w': normal((DEPTH, D, N_MOD * D), D, 0.5),
        'ada_b': small((DEPTH, N_MOD * D)),
        'norm_mix': gain((DEPTH, D)),
        'norm_ffn': gain((DEPTH, D)),
        'mla_wq_a': normal((n_mla, D, MLA_Q_LORA), D),
        'mla_q_norm': gain((n_mla, MLA_Q_LORA)),
        'mla_wq_b': normal((n_mla, MLA_Q_LORA, MLA_HEADS, MLA_QK), MLA_Q_LORA),
        'mla_wkv_a': normal((n_mla, D, MLA_KV_LORA + MLA_ROPE), D),
        'mla_kv_norm': gain((n_mla, MLA_KV_LORA)),
        'mla_wkv_b': normal((n_mla, MLA_KV_LORA, MLA_HEADS, MLA_NOPE + MLA_V), MLA_KV_LORA),
        'mla_wo': normal((n_mla, MLA_HEADS, MLA_V, D), MLA_HEADS * MLA_V),
        'ret_wq': normal((n_ret, D, RET_HEADS, RET_DK), D),
        'ret_wk': normal((n_ret, D, RET_HEADS, RET_DK), D),
        'ret_wv': normal((n_ret, D, RET_HEADS, RET_DV), D),
        'ret_wg': normal((n_ret, D, RET_HEADS * RET_DV), D),
        'ret_wo': normal((n_ret, RET_HEADS * RET_DV, D), RET_HEADS * RET_DV),
        'ret_decay': decay_base + small((n_ret, 2, RET_HEADS), 0.05),
        'conv_pw1': normal((n_conv, D, 2 * D), D),
        'conv_b1': small((n_conv, 2 * D)),
        'conv_dw': normal((n_conv, CONV_WIDTH, D), CONV_WIDTH),
        'conv_dw_b': small((n_conv, D)),
        'conv_ln_g': gain((n_conv, D)),
        'conv_ln_b': small((n_conv, D)),
        'conv_pw2': normal((n_conv, D, D), D),
        'conv_b2': small((n_conv, D)),
        'moe_wg_router': normal((DEPTH, D, MOE_GROUPS), D),
        'moe_bg_router': small((DEPTH, MOE_GROUPS), 0.01),
        'moe_we_router': normal((DEPTH, D, MOE_GROUPS, MOE_PER_GROUP), D),
        'moe_be_router': small((DEPTH, MOE_GROUPS, MOE_PER_GROUP), 0.01),
        'moe_w1': normal((DEPTH, MOE_EXPERTS, D, MOE_D_FF), D),
        'moe_w3': normal((DEPTH, MOE_EXPERTS, D, MOE_D_FF), D),
        'moe_w2': normal((DEPTH, MOE_EXPERTS, MOE_D_FF, D), MOE_D_FF),
        'final_norm': gain((D,)),
    }


def reference(x, c, ctx, c_ctx, ada_w, ada_b, norm_mix, norm_ffn,
              mla_wq_a, mla_q_norm, mla_wq_b, mla_wkv_a, mla_kv_norm, mla_wkv_b, mla_wo,
              ret_wq, ret_wk, ret_wv, ret_wg, ret_wo, ret_decay,
              conv_pw1, conv_b1, conv_dw, conv_dw_b, conv_ln_g, conv_ln_b, conv_pw2, conv_b2,
              moe_wg_router, moe_bg_router, moe_we_router, moe_be_router, moe_w1, moe_w3, moe_w2,
              final_norm):
    B, S, D = x.shape
    C = ctx.shape[1]
    rows = S // GRID_W
    mla_cos, mla_sin = axial_rope_tables(rows, MLA_ROPE)
    ret_cos, ret_sin = retention_tables(S)
    xc = ctx
    for i in range(DEPTH):
        kind, slot, last = i % N_MIXERS, i // N_MIXERS, i == DEPTH - 1
        mod_x = jnp.split((jax.nn.silu(c) @ ada_w[i] + ada_b[i])[:, None, :], N_MOD, axis=-1)
        mod_c = jnp.split((jax.nn.silu(c_ctx) @ ada_w[i] + ada_b[i])[None, None, :], N_MOD, axis=-1)
        hx = modulate(rms_norm(x, norm_mix[i]), mod_x[0], mod_x[1])
        hc = modulate(rms_norm(xc, norm_mix[i]), mod_c[0], mod_c[1])
        if kind == 0:
            oc, ox = mla_mixer(hc, hx, mla_wq_a[slot], mla_q_norm[slot], mla_wq_b[slot], mla_wkv_a[slot],
                               mla_kv_norm[slot], mla_wkv_b[slot], mla_wo[slot], mla_cos, mla_sin, not last)
        elif kind == 1:
            oc, ox = retention_mixer(hc, hx, ret_wq[slot], ret_wk[slot], ret_wv[slot], ret_wg[slot],
                                     ret_wo[slot], ret_decay[slot], ret_cos, ret_sin, not last)
        else:
            conv_args = (conv_pw1[slot], conv_b1[slot], conv_dw[slot], conv_dw_b[slot],
                         conv_ln_g[slot], conv_ln_b[slot], conv_pw2[slot], conv_b2[slot])
            ox = conformer_conv(hx, *conv_args)
            oc = None if last else conformer_conv(hc, *conv_args)
        x = x + mod_x[2] * ox
        hx = modulate(rms_norm(x, norm_ffn[i]), mod_x[3], mod_x[4])
        moe_args = (moe_wg_router[i], moe_bg_router[i], moe_we_router[i], moe_be_router[i],
                    moe_w1[i], moe_w3[i], moe_w2[i])
        if last:
            ox = hier_moe(hx.reshape(B * S, D), *moe_args).reshape(B, S, D)
        else:
            xc = xc + mod_c[2] * oc
            hc = modulate(rms_norm(xc, norm_ffn[i]), mod_c[3], mod_c[4])
            tok = jnp.concatenate([hc, hx], axis=1).reshape(B * (C + S), D)
            out = hier_moe(tok, *moe_args).reshape(B, C + S, D)
            xc = xc + mod_c[5] * out[:, :C]
            ox = out[:, C:]
        x = x + mod_x[5] * ox
    return rms_norm(x, final_norm)
```

```python
import functools
import math
from typing import NamedTuple

import jax
import jax.numpy as jnp
from jax import lax
from jax.experimental import pallas as pl
from jax.experimental.pallas import tpu as pltpu

F32 = jnp.float32
BF16 = jnp.bfloat16

GRID_W = 64
ROPE_BASE = 10000.0
RET_THETA_BASE = 10000.0
NORM_EPS = 1e-6
GROUP_NORM_EPS = 1e-5
N_MOD = 6
MOE_TOP_K = 2

LANES = 128
MOD_ROWS = 8
VMEM_LIMIT_BYTES = 56 * 1024 * 1024
LOG2E = math.log2(math.e)


class Layout(NamedTuple):
    batch: int
    seq: int
    ctx: int
    d: int
    tm: int

    @property
    def n_lat(self):
        return self.batch * self.seq

    @property
    def n_tok(self):
        return self.batch * (self.seq + self.ctx)

    @property
    def lat_tiles(self):
        return self.n_lat // self.tm

    @property
    def row_tiles(self):
        return self.n_tok // self.tm

    @property
    def tiles_per_batch(self):
        return self.seq // self.tm


def _params(n_axes):
    return pltpu.CompilerParams(dimension_semantics=("arbitrary",) * n_axes,
                                vmem_limit_bytes=VMEM_LIMIT_BYTES)


def _mod_row(lay, layer, i):
    return layer * MOD_ROWS + jnp.minimum(i // lay.tiles_per_batch, lay.batch)


def _silu(v):
    return v * jax.nn.sigmoid(v)


def _adaln_kernel(c_ref, w_ref, b_ref, o_ref):
    a = _silu(c_ref[...]).astype(BF16)
    o_ref[...] = jnp.dot(a, w_ref[...].astype(BF16), preferred_element_type=F32) + b_ref[...]


def _adaln(cond, ada_w, ada_b, tn):
    n_layers, d, n = ada_w.shape
    return pl.pallas_call(
        _adaln_kernel,
        grid=(n_layers, n // tn),
        in_specs=[pl.BlockSpec((MOD_ROWS, d), lambda l, j: (0, 0)),
                  pl.BlockSpec((None, d, tn), lambda l, j: (l, 0, j)),
                  pl.BlockSpec((None, 1, tn), lambda l, j: (l, 0, j))],
        out_specs=pl.BlockSpec((None, MOD_ROWS, tn), lambda l, j: (l, 0, j)),
        out_shape=jax.ShapeDtypeStruct((n_layers, MOD_ROWS, n), F32),
        compiler_params=_params(2),
        name="adaln",
    )(cond, ada_w, ada_b.reshape(n_layers, 1, n))


def _rms(x, g):
    return x * lax.rsqrt(jnp.mean(x * x, axis=-1, keepdims=True) + NORM_EPS) * g


def _normmod_kernel(t_ref, g_ref, sh_ref, sc_ref, h_ref):
    h = _rms(t_ref[...], g_ref[...]) * (1.0 + sc_ref[...]) + sh_ref[...]
    h_ref[...] = h.astype(BF16)


def _route(logits, n_groups, per_group):
    n_exp = n_groups * per_group
    lane = lax.broadcasted_iota(jnp.int32, logits.shape, 1)
    neg = jnp.float32(-jnp.inf)
    is_g = (lane >= n_exp) & (lane < n_exp + n_groups)
    lg = jnp.where(is_g, logits, neg)
    mg = jnp.max(lg, axis=1, keepdims=True)
    g_sel = jnp.min(jnp.where(lg == mg, lane, LANES), axis=1, keepdims=True) - n_exp
    pg_sel = 1.0 / jnp.sum(jnp.where(is_g, jnp.exp(lg - mg), 0.0), axis=1, keepdims=True)
    in_sel = (lane >= g_sel * per_group) & (lane < (g_sel + 1) * per_group)
    le = jnp.where(in_sel, logits, neg)
    m1 = jnp.max(le, axis=1, keepdims=True)
    i1 = jnp.min(jnp.where(le == m1, lane, LANES), axis=1, keepdims=True)
    le2 = jnp.where(lane == i1, neg, le)
    m2 = jnp.max(le2, axis=1, keepdims=True)
    i2 = jnp.min(jnp.where(le2 == m2, lane, LANES), axis=1, keepdims=True)
    e2 = jnp.exp(m2 - m1)
    w1 = pg_sel / (1.0 + e2)
    w2 = pg_sel * e2 / (1.0 + e2)
    return jnp.where(lane == i1, w1, 0.0) + jnp.where(lane == i2, w2, 0.0)


def _normmod_router_kernel(t_ref, g_ref, sh_ref, sc_ref, w2_ref, wh_ref, rb_ref, h_ref, gate_ref, *,
                           n_groups, per_group):
    h = _rms(t_ref[...], g_ref[...]) * (1.0 + sc_ref[...]) + sh_ref[...]
    h_hi = h.astype(BF16)
    h_ref[...] = h_hi
    h_lo = (h - h_hi.astype(F32)).astype(BF16)
    hh = jnp.dot(h_hi, w2_ref[...], preferred_element_type=F32)
    hl = jnp.dot(h_lo, wh_ref[...], preferred_element_type=F32)
    logits = hh[:, :LANES] + hh[:, LANES:] + hl + rb_ref[...]
    gate_ref[...] = _route(logits, n_groups, per_group)


def _mod_specs(lay, layer, shift_idx, scale_idx):
    d = lay.d
    return [pl.BlockSpec((None, 1, d), lambda i: (_mod_row(lay, layer, i), 0, shift_idx)),
            pl.BlockSpec((None, 1, d), lambda i: (_mod_row(lay, layer, i), 0, scale_idx))]


def _normmod(lay, tok, g, mods, layer, shift_idx, scale_idx):
    d, tm = lay.d, lay.tm
    return pl.pallas_call(
        _normmod_kernel,
        grid=(lay.row_tiles,),
        in_specs=[pl.BlockSpec((tm, d), lambda i: (i, 0)),
                  pl.BlockSpec((1, d), lambda i: (0, 0))] + _mod_specs(lay, layer, shift_idx, scale_idx),
        out_specs=pl.BlockSpec((tm, d), lambda i: (i, 0)),
        out_shape=jax.ShapeDtypeStruct((lay.n_tok, d), BF16),
        compiler_params=_params(1),
        name="normmod",
    )(tok, g.reshape(1, d), mods, mods)


def _normmod_router(lay, tok, g, mods, layer, shift_idx, scale_idx, wg_r, bg_r, we_r, be_r):
    d, tm = lay.d, lay.tm
    n_groups, per_group = we_r.shape[1], we_r.shape[2]
    n_exp = n_groups * per_group
    assert n_exp + n_groups <= LANES
    wr = jnp.concatenate([we_r.reshape(d, n_exp), wg_r], axis=1)
    wr = jnp.pad(wr, ((0, 0), (0, LANES - wr.shape[1])))
    rb = jnp.pad(jnp.concatenate([be_r.reshape(n_exp), bg_r]), (0, LANES - n_exp - n_groups)).reshape(1, LANES)
    w_hi = wr.astype(BF16)
    w_lo = (wr - w_hi.astype(F32)).astype(BF16)
    w2 = jnp.concatenate([w_hi, w_lo], axis=1)
    return pl.pallas_call(
        functools.partial(_normmod_router_kernel, n_groups=n_groups, per_group=per_group),
        grid=(lay.row_tiles,),
        in_specs=[pl.BlockSpec((tm, d), lambda i: (i, 0)),
                  pl.BlockSpec((1, d), lambda i: (0, 0))] + _mod_specs(lay, layer, shift_idx, scale_idx) + [
                  pl.BlockSpec((d, 2 * LANES), lambda i: (0, 0)),
                  pl.BlockSpec((d, LANES), lambda i: (0, 0)),
                  pl.BlockSpec((1, LANES), lambda i: (0, 0))],
        out_specs=[pl.BlockSpec((tm, d), lambda i: (i, 0)),
                   pl.BlockSpec((tm, LANES), lambda i: (i, 0))],
        out_shape=[jax.ShapeDtypeStruct((lay.n_tok, d), BF16),
                   jax.ShapeDtypeStruct((lay.n_tok, LANES), F32)],
        compiler_params=_params(1),
        name="normmod_router",
    )(tok, g.reshape(1, d), mods, mods, w2, w_hi, rb)


def _final_norm_kernel(t_ref, g_ref, o_ref):
    o_ref[...] = _rms(t_ref[...], g_ref[...])


def _final_norm(lay, tok, g):
    d, tm = lay.d, lay.tm
    return pl.pallas_call(
        _final_norm_kernel,
        grid=(lay.lat_tiles,),
        in_specs=[pl.BlockSpec((tm, d), lambda i: (i, 0)),
                  pl.BlockSpec((1, d), lambda i: (0, 0))],
        out_specs=pl.BlockSpec((tm, d), lambda i: (i, 0)),
        out_shape=jax.ShapeDtypeStruct((lay.n_lat, d), F32),
        compiler_params=_params(1),
        name="final_norm",
    )(tok, g.reshape(1, d))


def _fused_matmul_kernel(*refs, n_w, n_extra, n_out, epilogue):
    a_ref = refs[0]
    w_refs = refs[1:1 + n_w]
    e_refs = refs[1 + n_w:1 + n_w + n_extra]
    o_refs = refs[1 + n_w + n_extra:1 + n_w + n_extra + n_out]
    wb_refs = refs[1 + n_w + n_extra + n_out:]

    @pl.when(pl.program_id(1) == 0)
    def _():
        for w_ref, wb_ref in zip(w_refs, wb_refs):
            wb_ref[...] = w_ref[...].astype(BF16)

    a = a_ref[...]
    accs = [jnp.dot(a, wb_ref[...], preferred_element_type=F32) for wb_ref in wb_refs]
    epilogue(accs, e_refs, o_refs)


def _fused_matmul(a, w_specs, extras, outs, epilogue, *, tm, tn, n_col_blocks, name):
    m, k = a.shape
    kernel = functools.partial(_fused_matmul_kernel, n_w=len(w_specs), n_extra=len(extras), n_out=len(outs),
                               epilogue=epilogue)
    res = pl.pallas_call(
        kernel,
        grid=(n_col_blocks, m // tm),
        in_specs=[pl.BlockSpec((tm, k), lambda j, i: (i, 0))] + [s for _, s in w_specs] + [s for _, s in extras],
        out_specs=[s for _, s in outs],
        out_shape=[s for s, _ in outs],
        scratch_shapes=[pltpu.VMEM((k, tn), BF16) for _ in w_specs],
        compiler_params=_params(2),
        name=name,
    )(a, *[w for w, _ in w_specs], *[e for e, _ in extras])
    return res


def _w2d(w, tn):
    return (w, pl.BlockSpec((w.shape[0], tn), lambda j, i: (0, j)))


def _tile_spec(tm, tn):
    return pl.BlockSpec((tm, tn), lambda j, i: (i, j))


def _row_spec(tn):
    return pl.BlockSpec((1, tn), lambda j, i: (0, j))


def _ep_store(dtype):
    def ep(accs, e_refs, o_refs):
        o_refs[0][...] = accs[0].astype(dtype)
    return ep


def _linear(a, w, *, tm, tn, dtype, name):
    n = w.shape[1]
    return _fused_matmul(a, [_w2d(w, tn)], [], [(jax.ShapeDtypeStruct((a.shape[0], n), dtype), _tile_spec(tm, tn))],
                         _ep_store(dtype), tm=tm, tn=tn, n_col_blocks=n // tn, name=name)[0]


def _ep_residual(accs, e_refs, o_refs):
    tok_ref, gate_ref = e_refs
    o_refs[0][...] = tok_ref[...] + gate_ref[...] * accs[0]


def _ep_residual_bias(accs, e_refs, o_refs):
    tok_ref, gate_ref, b_ref = e_refs
    o_refs[0][...] = tok_ref[...] + gate_ref[...] * (accs[0] + b_ref[...])


def _residual_linear(lay, a, w, tok, mods, layer, gate_idx, *, tn, bias=None, name):
    d, tm = lay.d, lay.tm
    cols = d // tn
    extras = [(tok, _tile_spec(tm, tn)),
              (mods, pl.BlockSpec((None, 1, tn), lambda j, i: (_mod_row(lay, layer, i), 0, gate_idx * cols + j)))]
    ep = _ep_residual
    if bias is not None:
        extras.append((bias.reshape(1, d), _row_spec(tn)))
        ep = _ep_residual_bias
    return _fused_matmul(a, [_w2d(w, tn)], extras, [(jax.ShapeDtypeStruct((lay.n_tok, d), F32), _tile_spec(tm, tn))],
                         ep, tm=tm, tn=tn, n_col_blocks=cols, name=name)[0]


def _flat_positions_table(lay, lat_table, ctx_row):
    lat = jnp.tile(lat_table, (lay.batch, 1))
    ctx = jnp.broadcast_to(ctx_row, (lay.batch * lay.ctx, lat_table.shape[1]))
    return jnp.concatenate([lat, ctx], axis=0)


def _mla_rope_tables(lay, rope_dim):
    rows = lay.seq // GRID_W
    grid = jnp.stack(jnp.meshgrid(jnp.arange(rows), jnp.arange(GRID_W), indexing='ij'), axis=-1)
    grid = grid.reshape(-1, 2).astype(F32)
    n_freq = rope_dim // 4
    inv = ROPE_BASE ** (-jnp.arange(n_freq, dtype=F32) / n_freq)
    ang = jnp.concatenate([grid[:, :1] * inv, grid[:, 1:] * inv], axis=-1)
    cos, sin = jnp.cos(ang), jnp.sin(ang)
    pad = LANES - rope_dim
    cos_t = jnp.concatenate([cos, cos, jnp.ones((lay.seq, pad), F32)], axis=-1)
    sin_t = jnp.concatenate([-sin, sin, jnp.zeros((lay.seq, pad), F32)], axis=-1)
    one = jnp.ones((1, LANES), F32)
    return _flat_positions_table(lay, cos_t, one), _flat_positions_table(lay, sin_t, 0.0 * one)


def _rope_slab(r, cos_t, sin_t, half):
    lane = lax.broadcasted_iota(jnp.int32, r.shape, 1)
    partner = jnp.where(lane < half, pltpu.roll(r, LANES - half, 1), pltpu.roll(r, half, 1))
    return r * cos_t + partner * sin_t


def _ret_tables(lay, dk):
    theta = 1.0 / (RET_THETA_BASE ** jnp.linspace(0.0, 1.0, dk // 2, dtype=F32))
    ang = jnp.arange(lay.seq, dtype=F32)[:, None] * theta
    one = jnp.ones((1, dk // 2), F32)
    return (_flat_positions_table(lay, jnp.cos(ang), one),
            _flat_positions_table(lay, jnp.sin(ang), 0.0 * one))


def _mla_norm_kernel(cq_ref, kv_ref, qn_ref, kvn_ref, cos_ref, sin_ref, cqn_ref, ckv_ref, kr_ref, *, kv_lora, rope):
    cqn_ref[...] = _rms(cq_ref[...], qn_ref[...]).astype(BF16)
    kv = kv_ref[...]
    ckv_ref[...] = _rms(kv[:, :kv_lora], kvn_ref[...]).astype(BF16)
    kr_ref[...] = _rope_slab(kv[:, kv_lora:], cos_ref[...], sin_ref[...], rope // 2).astype(BF16)


def _mla_norm(lay, cq, kv, q_norm, kv_norm, cos_t, sin_t, kv_lora, rope):
    tm = lay.tm
    q_lora = cq.shape[1]
    row = lambda w: pl.BlockSpec((tm, w), lambda i: (i, 0))
    const = lambda w: pl.BlockSpec((1, w), lambda i: (0, 0))
    return pl.pallas_call(
        functools.partial(_mla_norm_kernel, kv_lora=kv_lora, rope=rope),
        grid=(lay.row_tiles,),
        in_specs=[row(q_lora), row(kv_lora + LANES), const(q_lora), const(kv_lora), row(LANES), row(LANES)],
        out_specs=[row(q_lora), row(kv_lora), row(LANES)],
        out_shape=[jax.ShapeDtypeStruct((lay.n_tok, q_lora), BF16),
                   jax.ShapeDtypeStruct((lay.n_tok, kv_lora), BF16),
                   jax.ShapeDtypeStruct((lay.n_tok, LANES), BF16)],
        compiler_params=_params(1),
        name="mla_norm",
    )(cq, kv, q_norm.reshape(1, q_lora), kv_norm.reshape(1, kv_lora), cos_t, sin_t)


def _ep_mla_q(accs, e_refs, o_refs, *, heads_per_tile, rope):
    cos_ref, sin_ref = e_refs
    acc = accs[0]
    cos_t, sin_t = cos_ref[...], sin_ref[...]
    for h in range(heads_per_tile):
        base = h * 2 * LANES
        o_refs[0][:, base:base + LANES] = acc[:, base:base + LANES].astype(BF16)
        slab = _rope_slab(acc[:, base + LANES:base + 2 * LANES], cos_t, sin_t, rope // 2)
        o_refs[0][:, base + LANES:base + 2 * LANES] = slab.astype(BF16)


def _ep_mla_k(accs, e_refs, o_refs, *, heads_per_tile):
    kr = e_refs[0][...]
    acc = accs[0]
    for h in range(heads_per_tile):
        o_refs[0][:, h * 2 * LANES:h * 2 * LANES + LANES] = acc[:, h * LANES:(h + 1) * LANES].astype(BF16)
        o_refs[0][:, h * 2 * LANES + LANES:(h + 1) * 2 * LANES] = kr


def _attn_kernel(q_ref, *refs, chunks, scale):
    o_ref = refs[-1]
    kv_refs = refs[:-1]
    q = q_ref[...]
    tq = q.shape[0]
    c = scale * LOG2E
    m = jnp.full((tq, 1), -jnp.inf, F32)
    l = jnp.zeros((tq, 1), F32)
    acc = jnp.zeros((tq, o_ref.shape[1]), F32)
    for ref_idx, start, size in chunks:
        k = kv_refs[2 * ref_idx][start:start + size, :]
        v = kv_refs[2 * ref_idx + 1][start:start + size, :]
        s = lax.dot_general(q, k, (((1,), (1,)), ((), ())), preferred_element_type=F32) * c
        m_new = jnp.maximum(m, jnp.max(s, axis=1, keepdims=True))
        alpha = jnp.exp2(m - m_new)
        p = jnp.exp2(s - m_new)
        l = alpha * l + jnp.sum(p, axis=1, keepdims=True)
        acc = alpha * acc + jnp.dot(p.astype(BF16), v, preferred_element_type=F32)
        m = m_new
    o_ref[...] = (acc / l).astype(o_ref.dtype)


def _mla_attention(lay, q, k, v, heads, scale, tq, tk):
    b, s, c = lay.batch, lay.seq, lay.ctx
    qk_w, v_w = 2 * LANES, LANES
    ctx_blk0 = lay.n_lat // c
    lat_chunks = tuple([(0, 0, c)] + [(1, st, tk) for st in range(0, s, tk)])
    nq = s // tq
    o_lat = pl.pallas_call(
        functools.partial(_attn_kernel, chunks=lat_chunks, scale=scale),
        grid=(b, heads, nq),
        in_specs=[pl.BlockSpec((tq, qk_w), lambda bi, h, qi: (bi * nq + qi, h)),
                  pl.BlockSpec((c, qk_w), lambda bi, h, qi: (ctx_blk0 + bi, h)),
                  pl.BlockSpec((c, v_w), lambda bi, h, qi: (ctx_blk0 + bi, h)),
                  pl.BlockSpec((s, qk_w), lambda bi, h, qi: (bi, h)),
                  pl.BlockSpec((s, v_w), lambda bi, h, qi: (bi, h))],
        out_specs=pl.BlockSpec((tq, v_w), lambda bi, h, qi: (bi * nq + qi, h)),
        out_shape=jax.ShapeDtypeStruct((lay.n_tok, heads * v_w), BF16),
        compiler_params=_params(3),
        name="mla_attn_latent",
    )(q, k, v, k, v)
    o_ctx = pl.pallas_call(
        functools.partial(_attn_kernel, chunks=((0, 0, c),), scale=scale),
        grid=(b, heads),
        in_specs=[pl.BlockSpec((c, qk_w), lambda bi, h: (ctx_blk0 + bi, h)),
                  pl.BlockSpec((c, qk_w), lambda bi, h: (ctx_blk0 + bi, h)),
                  pl.BlockSpec((c, v_w), lambda bi, h: (ctx_blk0 + bi, h))],
        out_specs=pl.BlockSpec((c, v_w), lambda bi, h: (bi, h)),
        out_shape=jax.ShapeDtypeStruct((b * c, heads * v_w), BF16),
        compiler_params=_params(2),
        name="mla_attn_ctx",
    )(q, k, v)
    return _insert_ctx_rows(lay, o_lat, o_ctx)


def _insert_kernel(big_ref, small_ref, o_ref):
    del big_ref
    o_ref[...] = small_ref[...]


def _insert_ctx_rows(lay, big, small):
    tm, w = lay.tm, big.shape[1]
    assert small.shape == (tm, w)
    return pl.pallas_call(
        _insert_kernel,
        grid=(1,),
        in_specs=[pl.BlockSpec(memory_space=pl.ANY),
                  pl.BlockSpec((tm, w), lambda i: (0, 0))],
        out_specs=pl.BlockSpec((tm, w), lambda i: (lay.lat_tiles, 0)),
        out_shape=jax.ShapeDtypeStruct(big.shape, big.dtype),
        input_output_aliases={0: 0},
        compiler_params=_params(1),
        name="insert_ctx_rows",
    )(big, small)


def _mla_mixer(lay, h, tok, mods, layer, wq_a, q_norm, wq_b, wkv_a, kv_norm, wkv_b, wo):
    d, tm = lay.d, lay.tm
    q_lora = wq_a.shape[1]
    heads, qk = wq_b.shape[1], wq_b.shape[2]
    kv_lora = kv_norm.shape[0]
    rope = wkv_a.shape[1] - kv_lora
    nope = qk - rope
    v_dim = wkv_b.shape[2] - nope
    assert nope == LANES and v_dim == LANES and rope <= LANES and rope % 4 == 0
    cos_t, sin_t = _mla_rope_tables(lay, rope)

    cq = _linear(h, wq_a, tm=tm, tn=min(512, q_lora), dtype=F32, name="mla_q_a")
    wkv_a_pad = jnp.pad(wkv_a, ((0, 0), (0, LANES - rope)))
    kv = _linear(h, wkv_a_pad, tm=tm, tn=kv_lora + LANES, dtype=F32, name="mla_kv_a")
    cqn, ckv, kr = _mla_norm(lay, cq, kv, q_norm, kv_norm, cos_t, sin_t, kv_lora, rope)

    hpt = 2
    wq = jnp.pad(wq_b, ((0, 0), (0, 0), (0, 2 * LANES - qk))).reshape(q_lora, heads * 2 * LANES)
    tn_q = hpt * 2 * LANES
    q = _fused_matmul(
        cqn, [_w2d(wq, tn_q)],
        [(cos_t, pl.BlockSpec((tm, LANES), lambda j, i: (i, 0))), (sin_t, pl.BlockSpec((tm, LANES), lambda j, i: (i, 0)))],
        [(jax.ShapeDtypeStruct((lay.n_tok, heads * 2 * LANES), BF16), _tile_spec(tm, tn_q))],
        functools.partial(_ep_mla_q, heads_per_tile=hpt, rope=rope),
        tm=tm, tn=tn_q, n_col_blocks=heads // hpt, name="mla_q_b")[0]

    wk = wkv_b[:, :, :nope].reshape(kv_lora, heads * nope)
    wv = wkv_b[:, :, nope:].reshape(kv_lora, heads * v_dim)
    hpt_k = 4
    k = _fused_matmul(
        ckv, [_w2d(wk, hpt_k * LANES)],
        [(kr, pl.BlockSpec((tm, LANES), lambda j, i: (i, 0)))],
        [(jax.ShapeDtypeStruct((lay.n_tok, heads * 2 * LANES), BF16), _tile_spec(tm, hpt_k * 2 * LANES))],
        functools.partial(_ep_mla_k, heads_per_tile=hpt_k),
        tm=tm, tn=hpt_k * LANES, n_col_blocks=heads // hpt_k, name="mla_k_b")[0]
    v = _linear(ckv, wv, tm=tm, tn=min(512, heads * v_dim), dtype=BF16, name="mla_v_b")

    o = _mla_attention(lay, q, k, v, heads, qk ** -0.5, tq=min(512, lay.seq), tk=min(512, lay.seq))
    return _residual_linear(lay, o, wo.reshape(heads * v_dim, d), tok, mods, layer, 2, tn=min(512, d), name="mla_o")


def _ep_rotate(accs, e_refs, o_refs, *, heads_per_tile, dk, scale):
    cos_ref, sin_ref = e_refs
    cos, sin = cos_ref[...], sin_ref[...]
    acc = accs[0]
    half = dk // 2
    for h in range(heads_per_tile):
        x1 = acc[:, h * dk:h * dk + half]
        x2 = acc[:, h * dk + half:(h + 1) * dk]
        o_refs[0][:, h * dk:h * dk + half] = ((x1 * cos - x2 * sin) * scale).astype(BF16)
        o_refs[0][:, h * dk + half:(h + 1) * dk] = ((x1 * sin + x2 * cos) * scale).astype(BF16)


def _rotated_linear(lay, a, w, cos, sin, dk, scale, name):
    tm = lay.tm
    n = w.shape[1]
    tn = 2 * dk
    half = dk // 2
    return _fused_matmul(
        a, [_w2d(w, tn)],
        [(cos, pl.BlockSpec((tm, half), lambda j, i: (i, 0))), (sin, pl.BlockSpec((tm, half), lambda j, i: (i, 0)))],
        [(jax.ShapeDtypeStruct((lay.n_tok, n), BF16), _tile_spec(tm, tn))],
        functools.partial(_ep_rotate, heads_per_tile=tn // dk, dk=dk, scale=scale),
        tm=tm, tn=tn, n_col_blocks=n // tn, name=name)[0]


def _retention_kernel(lg_ref, ql_ref, kl_ref, vl_ref, qc_ref, kc_ref, vc_ref, yl_ref, yc_ref,
                      sf_ref, sb_ref, yfl_ref, yfc_ref, ybl_ref, ybc_ref, *, chunk, heads):
    h = pl.program_id(1)
    lg_f = lg_ref[h]
    lg_b = lg_ref[heads + h]
    n_lat = ql_ref.shape[0] // chunk
    n_ctx = qc_ref.shape[0] // chunk
    dv = vl_ref.shape[1]

    row = lax.broadcasted_iota(jnp.int32, (chunk, chunk), 0).astype(F32)
    col = lax.broadcasted_iota(jnp.int32, (chunk, chunk), 1).astype(F32)
    pos = lax.broadcasted_iota(jnp.int32, (chunk, dv), 0).astype(F32)
    diff = row - col
    dmat_f = jnp.where(diff >= 0, jnp.exp(jnp.where(diff >= 0, diff, 0.0) * lg_f), 0.0)
    dmat_b = jnp.where(diff <= 0, jnp.exp(jnp.where(diff <= 0, -diff, 0.0) * lg_b), 0.0)
    qdec_f = jnp.exp((pos + 1.0) * lg_f)
    kdec_f = jnp.exp((chunk - 1.0 - pos) * lg_f)
    qdec_b = jnp.exp((chunk - pos) * lg_b)
    kdec_b = jnp.exp(pos * lg_b)
    blk_f = jnp.exp(chunk * lg_f)
    blk_b = jnp.exp(chunk * lg_b)

    def step(q, k, v, s_ref, dmat, qdec, kdec, blk):
        scores = lax.dot_general(q, k, (((1,), (1,)), ((), ())), preferred_element_type=F32) * dmat
        inner = jnp.dot(scores.astype(BF16), v, preferred_element_type=F32)
        state = s_ref[...]
        cross = jnp.dot(q, state.astype(BF16), preferred_element_type=F32) * qdec
        kd = (k.astype(F32) * kdec).astype(BF16)
        s_ref[...] = state * blk + lax.dot_general(kd, v, (((0,), (0,)), ((), ())), preferred_element_type=F32)
        return inner + cross

    def fwd(q_ref, k_ref, v_ref, y_ref, t):
        sl = pl.ds(pl.multiple_of(t * chunk, chunk), chunk)
        y_ref[sl, :] = step(q_ref[sl, :], k_ref[sl, :], v_ref[sl, :], sf_ref, dmat_f, qdec_f, kdec_f, blk_f)

    def bwd(q_ref, k_ref, v_ref, y_ref, t):
        sl = pl.ds(pl.multiple_of(t * chunk, chunk), chunk)
        y_ref[sl, :] = step(q_ref[sl, :], k_ref[sl, :], v_ref[sl, :], sb_ref, dmat_b, qdec_b, kdec_b, blk_b)

    sf_ref[...] = jnp.zeros_like(sf_ref)
    sb_ref[...] = jnp.zeros_like(sb_ref)
    for t in range(n_ctx):
        fwd(qc_ref, kc_ref, vc_ref, yfc_ref, t)
        bwd(qc_ref, kc_ref, vc_ref, ybc_ref, n_ctx - 1 - t)

    def body(t, carry):
        fwd(ql_ref, kl_ref, vl_ref, yfl_ref, t)
        bwd(ql_ref, kl_ref, vl_ref, ybl_ref, n_lat - 1 - t)
        return carry

    lax.fori_loop(0, n_lat, body, 0)

    def group_norm(y):
        mu = jnp.mean(y, axis=-1, keepdims=True)
        var = jnp.mean(jnp.square(y - mu), axis=-1, keepdims=True)
        return (y - mu) * lax.rsqrt(var + GROUP_NORM_EPS)

    for t in range(n_ctx):
        sl = pl.ds(t * chunk, chunk)
        yc_ref[sl, :] = group_norm(yfc_ref[sl, :] + ybc_ref[sl, :]).astype(BF16)

    def norm_body(t, carry):
        sl = pl.ds(pl.multiple_of(t * chunk, chunk), chunk)
        yl_ref[sl, :] = group_norm(yfl_ref[sl, :] + ybl_ref[sl, :]).astype(BF16)
        return carry

    lax.fori_loop(0, n_lat, norm_body, 0)


def _retention(lay, q, k, v, log_g, heads, dk, dv, chunk):
    b, s, c = lay.batch, lay.seq, lay.ctx
    ctx_blk0 = lay.n_lat // c
    lat = lambda w: pl.BlockSpec((s, w), lambda bi, h, lg: (bi, h))
    ctx = lambda w: pl.BlockSpec((c, w), lambda bi, h, lg: (ctx_blk0 + bi, h))
    y_lat, y_ctx = pl.pallas_call(
        functools.partial(_retention_kernel, chunk=chunk, heads=heads),
        grid_spec=pltpu.PrefetchScalarGridSpec(
            num_scalar_prefetch=1,
            grid=(b, heads),
            in_specs=[lat(dk), lat(dk), lat(dv), ctx(dk), ctx(dk), ctx(dv)],
            out_specs=[lat(dv), pl.BlockSpec((c, dv), lambda bi, h, lg: (bi, h))],
            scratch_shapes=[pltpu.VMEM((dk, dv), F32), pltpu.VMEM((dk, dv), F32),
                            pltpu.VMEM((s, dv), F32), pltpu.VMEM((c, dv), F32),
                            pltpu.VMEM((s, dv), F32), pltpu.VMEM((c, dv), F32)]),
        out_shape=[jax.ShapeDtypeStruct((lay.n_tok, heads * dv), BF16),
                   jax.ShapeDtypeStruct((b * c, heads * dv), BF16)],
        compiler_params=_params(2),
        name="retention",
    )(log_g.reshape(-1), q, k, v, q, k, v)
    return _insert_ctx_rows(lay, y_lat, y_ctx)


def _ep_gate_mul(accs, e_refs, o_refs):
    o_refs[0][...] = (_silu(accs[0]) * e_refs[0][...].astype(F32)).astype(BF16)


def _retention_mixer(lay, h, tok, mods, layer, wq, wk, wv, wg, wo, decay):
    d, tm = lay.d, lay.tm
    heads, dk = wq.shape[1], wq.shape[2]
    dv = wv.shape[2]
    assert dk == dv and dk % (2 * LANES) == 0
    chunk = min(256, lay.ctx)
    assert lay.ctx % chunk == 0 and lay.seq % chunk == 0
    log_g = -jnp.exp(decay.astype(F32))
    cos, sin = _ret_tables(lay, dk)
    q = _rotated_linear(lay, h, wq.reshape(d, heads * dk), cos, sin, dk, 1.0, "ret_q")
    k = _rotated_linear(lay, h, wk.reshape(d, heads * dk), cos, sin, dk, dk ** -0.5, "ret_k")
    v = _linear(h, wv.reshape(d, heads * dv), tm=tm, tn=min(512, heads * dv), dtype=BF16, name="ret_v")
    y = _retention(lay, q, k, v, log_g, heads, dk, dv, chunk)
    tn = min(512, heads * dv)
    z = _fused_matmul(h, [_w2d(wg, tn)], [(y, _tile_spec(tm, tn))],
                      [(jax.ShapeDtypeStruct((lay.n_tok, heads * dv), BF16), _tile_spec(tm, tn))],
                      _ep_gate_mul, tm=tm, tn=tn, n_col_blocks=heads * dv // tn, name="ret_gate")[0]
    return _residual_linear(lay, z, wo, tok, mods, layer, 2, tn=min(512, d), name="ret_o")


def _ep_glu(accs, e_refs, o_refs):
    ba_ref, bg_ref = e_refs
    o_refs[0][...] = (accs[0] + ba_ref[...]) * jax.nn.sigmoid(accs[1] + bg_ref[...])


HALO = 16


def _dwconv_kernel(prev_ref, cur_ref, next_ref, w_ref, wb_ref, g_ref, b_ref, o_ref, buf_ref, *,
                   width, tt, tiles_per_seq, tiles_per_ctx, lat_tiles):
    i = pl.program_id(0)
    in_lat = i < lat_tiles
    pos = jnp.where(in_lat, i % tiles_per_seq, (i - lat_tiles) % tiles_per_ctx)
    n_seq_tiles = jnp.where(in_lat, tiles_per_seq, tiles_per_ctx)
    first = pos == 0
    last = pos == n_seq_tiles - 1
    buf_ref[0:HALO, :] = jnp.where(first, 0.0, prev_ref[...])
    buf_ref[HALO:HALO + tt, :] = cur_ref[...]
    buf_ref[HALO + tt:HALO + tt + HALO, :] = jnp.where(last, 0.0, next_ref[...])
    pad = width // 2
    acc = jnp.zeros(cur_ref.shape, F32) + wb_ref[...]
    for kk in range(width):
        off = HALO - pad + kk
        acc = acc + buf_ref[off:off + tt, :] * w_ref[kk:kk + 1, :]
    mu = jnp.mean(acc, axis=-1, keepdims=True)
    var = jnp.mean(jnp.square(acc - mu), axis=-1, keepdims=True)
    y = (acc - mu) * lax.rsqrt(var + NORM_EPS) * g_ref[...] + b_ref[...]
    o_ref[...] = _silu(y).astype(BF16)


def _dwconv_ln_swish(lay, u, dw, dw_b, ln_g, ln_b):
    d = lay.d
    width = dw.shape[0]
    assert width // 2 <= HALO
    tt = min(256, lay.ctx)
    assert lay.ctx % tt == 0 and lay.seq % tt == 0 and tt % HALO == 0
    r = tt // HALO
    n_tiles = lay.n_tok // tt
    n_halo_blocks = lay.n_tok // HALO
    const = lambda rows: pl.BlockSpec((rows, d), lambda i: (0, 0))
    kern = functools.partial(_dwconv_kernel, width=width, tt=tt, tiles_per_seq=lay.seq // tt,
                             tiles_per_ctx=lay.ctx // tt, lat_tiles=lay.n_lat // tt)
    return pl.pallas_call(
        kern,
        grid=(n_tiles,),
        in_specs=[pl.BlockSpec((HALO, d), lambda i: (jnp.maximum(i * r - 1, 0), 0)),
                  pl.BlockSpec((tt, d), lambda i: (i, 0)),
                  pl.BlockSpec((HALO, d), lambda i: (jnp.minimum((i + 1) * r, n_halo_blocks - 1), 0)),
                  const(width), const(1), const(1), const(1)],
        out_specs=pl.BlockSpec((tt, d), lambda i: (i, 0)),
        out_shape=jax.ShapeDtypeStruct((lay.n_tok, d), BF16),
        scratch_shapes=[pltpu.VMEM((tt + 2 * HALO, d), F32)],
        compiler_params=_params(1),
        name="dwconv_ln_swish",
    )(u, u, u, dw, dw_b.reshape(1, d), ln_g.reshape(1, d), ln_b.reshape(1, d))


def _conformer_mixer(lay, h, tok, mods, layer, pw1, b1, dw, dw_b, ln_g, ln_b, pw2, b2):
    d, tm = lay.d, lay.tm
    tn = min(256, d)
    cols = d // tn
    b1r = b1.reshape(1, 2 * d)
    u = _fused_matmul(
        h,
        [(pw1, pl.BlockSpec((d, tn), lambda j, i: (0, j))), (pw1, pl.BlockSpec((d, tn), lambda j, i: (0, cols + j)))],
        [(b1r, pl.BlockSpec((1, tn), lambda j, i: (0, j))), (b1r, pl.BlockSpec((1, tn), lambda j, i: (0, cols + j)))],
        [(jax.ShapeDtypeStruct((lay.n_tok, d), F32), _tile_spec(tm, tn))],
        _ep_glu, tm=tm, tn=tn, n_col_blocks=cols, name="conv_pw1_glu")[0]
    a = _dwconv_ln_swish(lay, u, dw, dw_b, ln_g, ln_b)
    return _residual_linear(lay, a, pw2, tok, mods, layer, 2, tn=min(512, d), bias=b2, name="conv_pw2")


def _ep_expert_up(accs, e_refs, o_refs):
    gates = e_refs[0][...]
    e = pl.program_id(0)
    lane = lax.broadcasted_iota(jnp.int32, gates.shape, 1)
    gate = jnp.sum(jnp.where(lane == e, gates, 0.0), axis=1, keepdims=True)
    o_refs[0][...] = (_silu(accs[0]) * accs[1] * gate).astype(BF16)


def _moe_dense(lay, h, gates, tok, mods, layer, w1, w3, w2):
    d, tm = lay.d, lay.tm
    n_exp, _, d_ff = w1.shape
    wspec = pl.BlockSpec((None, d, d_ff), lambda j, i: (j, 0, 0))
    act = _fused_matmul(
        h, [(w1, wspec), (w3, wspec)],
        [(gates, pl.BlockSpec((tm, LANES), lambda j, i: (i, 0)))],
        [(jax.ShapeDtypeStruct((lay.n_tok, n_exp * d_ff), BF16), _tile_spec(tm, d_ff))],
        _ep_expert_up, tm=tm, tn=d_ff, n_col_blocks=n_exp, name="moe_up")[0]
    return _residual_linear(lay, act, w2.reshape(n_exp * d_ff, d), tok, mods, layer, 5, tn=min(256, d), name="moe_down")


def kernel(x, c, ctx, c_ctx, ada_w, ada_b, norm_mix, norm_ffn, mla_wq_a, mla_q_norm, mla_wq_b, mla_wkv_a, mla_kv_norm, mla_wkv_b, mla_wo, ret_wq, ret_wk, ret_wv, ret_wg, ret_wo, ret_decay, conv_pw1, conv_b1, conv_dw, conv_dw_b, conv_ln_g, conv_ln_b, conv_pw2, conv_b2, moe_wg_router, moe_bg_router, moe_we_router, moe_be_router, moe_w1, moe_w3, moe_w2, final_norm):
    b, s, d = x.shape
    n_ctx = ctx.shape[1]
    depth = ada_w.shape[0]
    n_mixers = 3
    lay = Layout(batch=b, seq=s, ctx=n_ctx, d=d, tm=b * n_ctx)
    assert s % lay.tm == 0 and b + 1 <= MOD_ROWS and s % GRID_W == 0

    tok = jnp.concatenate([x.reshape(b * s, d), ctx.reshape(b * n_ctx, d)], axis=0)
    cond = jnp.zeros((MOD_ROWS, d), F32).at[:b].set(c).at[b].set(c_ctx)
    mods = _adaln(cond, ada_w, ada_b, tn=min(1024, d)).reshape(depth * MOD_ROWS, 1, N_MOD * d)

    for i in range(depth):
        kind, slot = i % n_mixers, i // n_mixers
        h = _normmod(lay, tok, norm_mix[i], mods, i, 0, 1)
        if kind == 0:
            tok = _mla_mixer(lay, h, tok, mods, i, mla_wq_a[slot], mla_q_norm[slot], mla_wq_b[slot], mla_wkv_a[slot],
                             mla_kv_norm[slot], mla_wkv_b[slot], mla_wo[slot])
        elif kind == 1:
            tok = _retention_mixer(lay, h, tok, mods, i, ret_wq[slot], ret_wk[slot], ret_wv[slot], ret_wg[slot],
                                   ret_wo[slot], ret_decay[slot])
        else:
            tok = _conformer_mixer(lay, h, tok, mods, i, conv_pw1[slot], conv_b1[slot], conv_dw[slot], conv_dw_b[slot],
                                   conv_ln_g[slot], conv_ln_b[slot], conv_pw2[slot], conv_b2[slot])
        h, gates = _normmod_router(lay, tok, norm_ffn[i], mods, i, 3, 4, moe_wg_router[i], moe_bg_router[i],
                                   moe_we_router[i], moe_be_router[i])
        tok = _moe_dense(lay, h, gates, tok, mods, i, moe_w1[i], moe_w3[i], moe_w2[i])
    return _final_norm(lay, tok, final_norm).reshape(b, s, d)
```

```python
import functools
import math
from typing import NamedTuple

import jax
import jax.numpy as jnp
from jax import lax
from jax.experimental import pallas as pl
from jax.experimental.pallas import tpu as pltpu

F32 = jnp.float32
BF16 = jnp.bfloat16

GRID_W = 64
ROPE_BASE = 10000.0
RET_THETA_BASE = 10000.0
NORM_EPS = 1e-6
GROUP_NORM_EPS = 1e-5
N_MOD = 6
MOE_TOP_K = 2

LANES = 128
MOD_ROWS = 8
VMEM_LIMIT_BYTES = 56 * 1024 * 1024
LOG2E = math.log2(math.e)


class Layout(NamedTuple):
    batch: int
    seq: int
    ctx: int
    d: int
    tm: int

    @property
    def n_lat(self):
        return self.batch * self.seq

    @property
    def n_tok(self):
        return self.batch * (self.seq + self.ctx)

    @property
    def lat_tiles(self):
        return self.n_lat // self.tm

    @property
    def row_tiles(self):
        return self.n_tok // self.tm

    @property
    def tiles_per_batch(self):
        return self.seq // self.tm


def _params(n_axes):
    return pltpu.CompilerParams(dimension_semantics=("arbitrary",) * n_axes,
                                vmem_limit_bytes=VMEM_LIMIT_BYTES)


def _mod_row(lay, layer, i):
    return layer * MOD_ROWS + jnp.minimum(i // lay.tiles_per_batch, lay.batch)


def _silu(v):
    return v * jax.nn.sigmoid(v)


def _adaln_kernel(c_ref, w_ref, b_ref, o_ref):
    a = _silu(c_ref[...]).astype(BF16)
    o_ref[...] = jnp.dot(a, w_ref[...].astype(BF16), preferred_element_type=F32) + b_ref[...]


def _adaln(cond, ada_w, ada_b, tn):
    n_layers, d, n = ada_w.shape
    return pl.pallas_call(
        _adaln_kernel,
        grid=(n_layers, n // tn),
        in_specs=[pl.BlockSpec((MOD_ROWS, d), lambda l, j: (0, 0)),
                  pl.BlockSpec((None, d, tn), lambda l, j: (l, 0, j)),
                  pl.BlockSpec((None, 1, tn), lambda l, j: (l, 0, j))],
        out_specs=pl.BlockSpec((None, MOD_ROWS, tn), lambda l, j: (l, 0, j)),
        out_shape=jax.ShapeDtypeStruct((n_layers, MOD_ROWS, n), F32),
        compiler_params=_params(2),
        name="adaln",
    )(cond, ada_w, ada_b.reshape(n_layers, 1, n))


def _rms(x, g):
    return x * lax.rsqrt(jnp.mean(x * x, axis=-1, keepdims=True) + NORM_EPS) * g


def _normmod_kernel(t_ref, g_ref, sh_ref, sc_ref, h_ref):
    h = _rms(t_ref[...], g_ref[...]) * (1.0 + sc_ref[...]) + sh_ref[...]
    h_ref[...] = h.astype(BF16)


def _route(logits, n_groups, per_group):
    n_exp = n_groups * per_group
    lane = lax.broadcasted_iota(jnp.int32, logits.shape, 1)
    neg = jnp.float32(-jnp.inf)
    is_g = (lane >= n_exp) & (lane < n_exp + n_groups)
    lg = jnp.where(is_g, logits, neg)
    mg = jnp.max(lg, axis=1, keepdims=True)
    g_sel = jnp.min(jnp.where(lg == mg, lane, LANES), axis=1, keepdims=True) - n_exp
    pg_sel = 1.0 / jnp.sum(jnp.where(is_g, jnp.exp(lg - mg), 0.0), axis=1, keepdims=True)
    in_sel = (lane >= g_sel * per_group) & (lane < (g_sel + 1) * per_group)
    le = jnp.where(in_sel, logits, neg)
    m1 = jnp.max(le, axis=1, keepdims=True)
    i1 = jnp.min(jnp.where(le == m1, lane, LANES), axis=1, keepdims=True)
    le2 = jnp.where(lane == i1, neg, le)
    m2 = jnp.max(le2, axis=1, keepdims=True)
    i2 = jnp.min(jnp.where(le2 == m2, lane, LANES), axis=1, keepdims=True)
    e2 = jnp.exp(m2 - m1)
    w1 = pg_sel / (1.0 + e2)
    w2 = pg_sel * e2 / (1.0 + e2)
    idx = jnp.where(lane == 0, i1, jnp.where(lane == 1, i2, 0))
    wts = jnp.where(lane == 0, w1, jnp.where(lane == 1, w2, 0.0))
    return idx, wts


def _pack_bf16_pairs(h):
    half = h.shape[1] // 2
    hb = h.astype(BF16).astype(F32)
    lo = lax.shift_right_logical(lax.bitcast_convert_type(hb[:, :half], jnp.uint32), jnp.uint32(16))
    hi = lax.bitcast_convert_type(hb[:, half:], jnp.uint32) & jnp.uint32(0xFFFF0000)
    return hi | lo


def _unpack_bf16_pairs(w):
    lo = lax.bitcast_convert_type(lax.shift_left(w, jnp.uint32(16)), F32)
    hi = lax.bitcast_convert_type(w & jnp.uint32(0xFFFF0000), F32)
    return lo.astype(BF16), hi.astype(BF16)


def _normmod_router_kernel(t_ref, g_ref, sh_ref, sc_ref, w2_ref, wh_ref, rb_ref, hp_ref, idx_ref, wts_ref, *,
                           n_groups, per_group):
    h = _rms(t_ref[...], g_ref[...]) * (1.0 + sc_ref[...]) + sh_ref[...]
    hp_ref[...] = _pack_bf16_pairs(h)
    h_hi = h.astype(BF16)
    h_lo = (h - h_hi.astype(F32)).astype(BF16)
    hh = jnp.dot(h_hi, w2_ref[...], preferred_element_type=F32)
    hl = jnp.dot(h_lo, wh_ref[...], preferred_element_type=F32)
    logits = hh[:, :LANES] + hh[:, LANES:] + hl + rb_ref[...]
    idx, wts = _route(logits, n_groups, per_group)
    idx_ref[...] = idx
    wts_ref[...] = wts


def _mod_specs(lay, layer, shift_idx, scale_idx):
    d = lay.d
    return [pl.BlockSpec((None, 1, d), lambda i: (_mod_row(lay, layer, i), 0, shift_idx)),
            pl.BlockSpec((None, 1, d), lambda i: (_mod_row(lay, layer, i), 0, scale_idx))]


def _normmod(lay, tok, g, mods, layer, shift_idx, scale_idx):
    d, tm = lay.d, lay.tm
    return pl.pallas_call(
        _normmod_kernel,
        grid=(lay.row_tiles,),
        in_specs=[pl.BlockSpec((tm, d), lambda i: (i, 0)),
                  pl.BlockSpec((1, d), lambda i: (0, 0))] + _mod_specs(lay, layer, shift_idx, scale_idx),
        out_specs=pl.BlockSpec((tm, d), lambda i: (i, 0)),
        out_shape=jax.ShapeDtypeStruct((lay.n_tok, d), BF16),
        compiler_params=_params(1),
        name="normmod",
    )(tok, g.reshape(1, d), mods, mods)


def _normmod_router(lay, tok, g, mods, layer, shift_idx, scale_idx, wg_r, bg_r, we_r, be_r):
    d, tm = lay.d, lay.tm
    n_groups, per_group = we_r.shape[1], we_r.shape[2]
    n_exp = n_groups * per_group
    assert n_exp + n_groups <= LANES
    wr = jnp.concatenate([we_r.reshape(d, n_exp), wg_r], axis=1)
    wr = jnp.pad(wr, ((0, 0), (0, LANES - wr.shape[1])))
    rb = jnp.pad(jnp.concatenate([be_r.reshape(n_exp), bg_r]), (0, LANES - n_exp - n_groups)).reshape(1, LANES)
    w_hi = wr.astype(BF16)
    w_lo = (wr - w_hi.astype(F32)).astype(BF16)
    w2 = jnp.concatenate([w_hi, w_lo], axis=1)
    return pl.pallas_call(
        functools.partial(_normmod_router_kernel, n_groups=n_groups, per_group=per_group),
        grid=(lay.row_tiles,),
        in_specs=[pl.BlockSpec((tm, d), lambda i: (i, 0)),
                  pl.BlockSpec((1, d), lambda i: (0, 0))] + _mod_specs(lay, layer, shift_idx, scale_idx) + [
                  pl.BlockSpec((d, 2 * LANES), lambda i: (0, 0)),
                  pl.BlockSpec((d, LANES), lambda i: (0, 0)),
                  pl.BlockSpec((1, LANES), lambda i: (0, 0))],
        out_specs=[pl.BlockSpec((tm, d // 2), lambda i: (i, 0)),
                   pl.BlockSpec((tm, LANES), lambda i: (i, 0)),
                   pl.BlockSpec((tm, LANES), lambda i: (i, 0))],
        out_shape=[jax.ShapeDtypeStruct((lay.n_tok, d // 2), jnp.uint32),
                   jax.ShapeDtypeStruct((lay.n_tok, LANES), jnp.int32),
                   jax.ShapeDtypeStruct((lay.n_tok, LANES), F32)],
        compiler_params=_params(1),
        name="normmod_router",
    )(tok, g.reshape(1, d), mods, mods, w2, w_hi, rb)


def _final_norm_kernel(t_ref, g_ref, o_ref):
    o_ref[...] = _rms(t_ref[...], g_ref[...])


def _final_norm(lay, tok, g):
    d, tm = lay.d, lay.tm
    return pl.pallas_call(
        _final_norm_kernel,
        grid=(lay.lat_tiles,),
        in_specs=[pl.BlockSpec((tm, d), lambda i: (i, 0)),
                  pl.BlockSpec((1, d), lambda i: (0, 0))],
        out_specs=pl.BlockSpec((tm, d), lambda i: (i, 0)),
        out_shape=jax.ShapeDtypeStruct((lay.n_lat, d), F32),
        compiler_params=_params(1),
        name="final_norm",
    )(tok, g.reshape(1, d))


def _fused_matmul_kernel(*refs, n_a, n_w, n_extra, n_out, epilogue, lat_tiles):
    a_refs = refs[:n_a]
    w_refs = refs[n_a:n_a + n_w]
    e_refs = refs[n_a + n_w:n_a + n_w + n_extra]
    o_refs = refs[n_a + n_w + n_extra:n_a + n_w + n_extra + n_out]
    wb_refs = refs[n_a + n_w + n_extra + n_out:]

    @pl.when(pl.program_id(1) == 0)
    def _():
        for w_ref, wb_ref in zip(w_refs, wb_refs):
            wb_ref[...] = w_ref[...].astype(BF16)

    def run(a_ref):
        a = a_ref[...]
        accs = [jnp.dot(a, wb_ref[...], preferred_element_type=F32) for wb_ref in wb_refs]
        epilogue(accs, e_refs, o_refs)

    if n_a == 1:
        run(a_refs[0])
    else:
        is_lat = pl.program_id(1) < lat_tiles
        pl.when(is_lat)(lambda: run(a_refs[0]))
        pl.when(jnp.logical_not(is_lat))(lambda: run(a_refs[1]))


def _fused_matmul(a, w_specs, extras, outs, epilogue, *, tm, tn, n_col_blocks, name):
    if isinstance(a, tuple):
        a_lat, a_ctx = a
        k = a_lat.shape[1]
        lat_tiles = a_lat.shape[0] // tm
        assert a_ctx.shape == (tm, k)
        a_args = [a_lat, a_ctx]
        a_specs = [pl.BlockSpec((tm, k), lambda j, i: (jnp.minimum(i, lat_tiles - 1), 0)),
                   pl.BlockSpec((tm, k), lambda j, i: (0, 0))]
        row_tiles = lat_tiles + 1
    else:
        m, k = a.shape
        lat_tiles = None
        a_args = [a]
        a_specs = [pl.BlockSpec((tm, k), lambda j, i: (i, 0))]
        row_tiles = m // tm
    kernel = functools.partial(_fused_matmul_kernel, n_a=len(a_args), n_w=len(w_specs), n_extra=len(extras),
                               n_out=len(outs), epilogue=epilogue, lat_tiles=lat_tiles)
    res = pl.pallas_call(
        kernel,
        grid=(n_col_blocks, row_tiles),
        in_specs=a_specs + [s for _, s in w_specs] + [s for _, s in extras],
        out_specs=[s for _, s in outs],
        out_shape=[s for s, _ in outs],
        scratch_shapes=[pltpu.VMEM((k, tn), BF16) for _ in w_specs],
        compiler_params=_params(2),
        name=name,
    )(*a_args, *[w for w, _ in w_specs], *[e for e, _ in extras])
    return res


def _w2d(w, tn):
    if isinstance(w, tuple):
        stack, slot = w
        return (stack, pl.BlockSpec((None, stack.shape[1], tn), lambda j, i: (slot, 0, j)))
    return (w, pl.BlockSpec((w.shape[0], tn), lambda j, i: (0, j)))


def _tile_spec(tm, tn):
    return pl.BlockSpec((tm, tn), lambda j, i: (i, j))


def _row_spec(tn):
    return pl.BlockSpec((1, tn), lambda j, i: (0, j))


def _ep_store(dtype):
    def ep(accs, e_refs, o_refs):
        o_refs[0][...] = accs[0].astype(dtype)
    return ep


def _linear(a, w, *, tm, tn, dtype, name):
    n = w[0].shape[2] if isinstance(w, tuple) else w.shape[1]
    return _fused_matmul(a, [_w2d(w, tn)], [], [(jax.ShapeDtypeStruct((a.shape[0], n), dtype), _tile_spec(tm, tn))],
                         _ep_store(dtype), tm=tm, tn=tn, n_col_blocks=n // tn, name=name)[0]


def _ep_residual(accs, e_refs, o_refs):
    tok_ref, gate_ref = e_refs
    o_refs[0][...] = tok_ref[...] + gate_ref[...] * accs[0]


def _ep_residual_bias(accs, e_refs, o_refs):
    tok_ref, gate_ref, b_ref = e_refs
    o_refs[0][...] = tok_ref[...] + gate_ref[...] * (accs[0] + b_ref[...])


def _residual_linear(lay, a, w, tok, mods, layer, gate_idx, *, tn, bias=None, name):
    d, tm = lay.d, lay.tm
    cols = d // tn
    extras = [(tok, _tile_spec(tm, tn)),
              (mods, pl.BlockSpec((None, 1, tn), lambda j, i: (_mod_row(lay, layer, i), 0, gate_idx * cols + j)))]
    ep = _ep_residual
    if bias is not None:
        extras.append((bias.reshape(1, d), _row_spec(tn)))
        ep = _ep_residual_bias
    return _fused_matmul(a, [_w2d(w, tn)], extras, [(jax.ShapeDtypeStruct((lay.n_tok, d), F32), _tile_spec(tm, tn))],
                         ep, tm=tm, tn=tn, n_col_blocks=cols, name=name)[0]


def _flat_positions_table(lay, lat_table, ctx_row):
    lat = jnp.tile(lat_table, (lay.batch, 1))
    ctx = jnp.broadcast_to(ctx_row, (lay.batch * lay.ctx, lat_table.shape[1]))
    return jnp.concatenate([lat, ctx], axis=0)


def _mla_rope_tables(lay, rope_dim):
    rows = lay.seq // GRID_W
    grid = jnp.stack(jnp.meshgrid(jnp.arange(rows), jnp.arange(GRID_W), indexing='ij'), axis=-1)
    grid = grid.reshape(-1, 2).astype(F32)
    n_freq = rope_dim // 4
    inv = ROPE_BASE ** (-jnp.arange(n_freq, dtype=F32) / n_freq)
    ang = jnp.concatenate([grid[:, :1] * inv, grid[:, 1:] * inv], axis=-1)
    cos, sin = jnp.cos(ang), jnp.sin(ang)
    pad = LANES - rope_dim
    cos_t = jnp.concatenate([cos, cos, jnp.ones((lay.seq, pad), F32)], axis=-1)
    sin_t = jnp.concatenate([-sin, sin, jnp.zeros((lay.seq, pad), F32)], axis=-1)
    one = jnp.ones((1, LANES), F32)
    return _flat_positions_table(lay, cos_t, one), _flat_positions_table(lay, sin_t, 0.0 * one)


def _rope_slab(r, cos_t, sin_t, half):
    lane = lax.broadcasted_iota(jnp.int32, r.shape, 1)
    partner = jnp.where(lane < half, pltpu.roll(r, LANES - half, 1), pltpu.roll(r, half, 1))
    return r * cos_t + partner * sin_t


def _ret_tables(lay, dk):
    theta = 1.0 / (RET_THETA_BASE ** jnp.linspace(0.0, 1.0, dk // 2, dtype=F32))
    ang = jnp.arange(lay.seq, dtype=F32)[:, None] * theta
    one = jnp.ones((1, dk // 2), F32)
    return (_flat_positions_table(lay, jnp.cos(ang), one),
            _flat_positions_table(lay, jnp.sin(ang), 0.0 * one))


def _mla_norm_kernel(cq_ref, kv_ref, qn_ref, kvn_ref, cos_ref, sin_ref, cqn_ref, ckv_ref, kr_ref, *, kv_lora, rope):
    cqn_ref[...] = _rms(cq_ref[...], qn_ref[...]).astype(BF16)
    kv = kv_ref[...]
    ckv_ref[...] = _rms(kv[:, :kv_lora], kvn_ref[...]).astype(BF16)
    kr_ref[...] = _rope_slab(kv[:, kv_lora:], cos_ref[...], sin_ref[...], rope // 2).astype(BF16)


def _mla_norm(lay, cq, kv, q_norm, kv_norm, cos_t, sin_t, kv_lora, rope):
    tm = lay.tm
    q_lora = cq.shape[1]
    row = lambda w: pl.BlockSpec((tm, w), lambda i: (i, 0))
    const = lambda w: pl.BlockSpec((1, w), lambda i: (0, 0))
    return pl.pallas_call(
        functools.partial(_mla_norm_kernel, kv_lora=kv_lora, rope=rope),
        grid=(lay.row_tiles,),
        in_specs=[row(q_lora), row(kv_lora + LANES), const(q_lora), const(kv_lora), row(LANES), row(LANES)],
        out_specs=[row(q_lora), row(kv_lora), row(LANES)],
        out_shape=[jax.ShapeDtypeStruct((lay.n_tok, q_lora), BF16),
                   jax.ShapeDtypeStruct((lay.n_tok, kv_lora), BF16),
                   jax.ShapeDtypeStruct((lay.n_tok, LANES), BF16)],
        compiler_params=_params(1),
        name="mla_norm",
    )(cq, kv, q_norm.reshape(1, q_lora), kv_norm.reshape(1, kv_lora), cos_t, sin_t)


def _ep_mla_q(accs, e_refs, o_refs, *, heads_per_tile, rope):
    cos_ref, sin_ref = e_refs
    acc = accs[0]
    cos_t, sin_t = cos_ref[...], sin_ref[...]
    for h in range(heads_per_tile):
        base = h * 2 * LANES
        o_refs[0][:, base:base + LANES] = acc[:, base:base + LANES].astype(BF16)
        slab = _rope_slab(acc[:, base + LANES:base + 2 * LANES], cos_t, sin_t, rope // 2)
        o_refs[0][:, base + LANES:base + 2 * LANES] = slab.astype(BF16)


def _ep_mla_k(accs, e_refs, o_refs, *, heads_per_tile):
    kr = e_refs[0][...]
    acc = accs[0]
    for h in range(heads_per_tile):
        o_refs[0][:, h * 2 * LANES:h * 2 * LANES + LANES] = acc[:, h * LANES:(h + 1) * LANES].astype(BF16)
        o_refs[0][:, h * 2 * LANES + LANES:(h + 1) * 2 * LANES] = kr


def _attn_kernel(q_ref, *refs, chunks, scale):
    o_ref = refs[-1]
    kv_refs = refs[:-1]
    q = q_ref[...]
    tq = q.shape[0]
    c = scale * LOG2E
    m = jnp.full((tq, 1), -jnp.inf, F32)
    l = jnp.zeros((tq, 1), F32)
    acc = jnp.zeros((tq, o_ref.shape[1]), F32)
    for ref_idx, start, size in chunks:
        k = kv_refs[2 * ref_idx][start:start + size, :]
        v = kv_refs[2 * ref_idx + 1][start:start + size, :]
        s = lax.dot_general(q, k, (((1,), (1,)), ((), ())), preferred_element_type=F32) * c
        m_new = jnp.maximum(m, jnp.max(s, axis=1, keepdims=True))
        alpha = jnp.exp2(m - m_new)
        p = jnp.exp2(s - m_new)
        l = alpha * l + jnp.sum(p, axis=1, keepdims=True)
        acc = alpha * acc + jnp.dot(p.astype(BF16), v, preferred_element_type=F32)
        m = m_new
    o_ref[...] = (acc / l).astype(o_ref.dtype)


def _mla_attention(lay, q, k, v, heads, scale, tq, tk):
    b, s, c = lay.batch, lay.seq, lay.ctx
    qk_w, v_w = 2 * LANES, LANES
    ctx_blk0 = lay.n_lat // c
    lat_chunks = tuple([(0, 0, c)] + [(1, st, tk) for st in range(0, s, tk)])
    nq = s // tq
    o_lat = pl.pallas_call(
        functools.partial(_attn_kernel, chunks=lat_chunks, scale=scale),
        grid=(b, heads, nq),
        in_specs=[pl.BlockSpec((tq, qk_w), lambda bi, h, qi: (bi * nq + qi, h)),
                  pl.BlockSpec((c, qk_w), lambda bi, h, qi: (ctx_blk0 + bi, h)),
                  pl.BlockSpec((c, v_w), lambda bi, h, qi: (ctx_blk0 + bi, h)),
                  pl.BlockSpec((s, qk_w), lambda bi, h, qi: (bi, h)),
                  pl.BlockSpec((s, v_w), lambda bi, h, qi: (bi, h))],
        out_specs=pl.BlockSpec((tq, v_w), lambda bi, h, qi: (bi * nq + qi, h)),
        out_shape=jax.ShapeDtypeStruct((lay.n_lat, heads * v_w), BF16),
        compiler_params=_params(3),
        name="mla_attn_latent",
    )(q, k, v, k, v)
    o_ctx = pl.pallas_call(
        functools.partial(_attn_kernel, chunks=((0, 0, c),), scale=scale),
        grid=(b, heads),
        in_specs=[pl.BlockSpec((c, qk_w), lambda bi, h: (ctx_blk0 + bi, h)),
                  pl.BlockSpec((c, qk_w), lambda bi, h: (ctx_blk0 + bi, h)),
                  pl.BlockSpec((c, v_w), lambda bi, h: (ctx_blk0 + bi, h))],
        out_specs=pl.BlockSpec((c, v_w), lambda bi, h: (bi, h)),
        out_shape=jax.ShapeDtypeStruct((b * c, heads * v_w), BF16),
        compiler_params=_params(2),
        name="mla_attn_ctx",
    )(q, k, v)
    return o_lat, o_ctx


def _mla_mixer(lay, h, tok, mods, layer, slot, wq_a_all, q_norm, wq_b, wkv_a, kv_norm, wkv_b, wo_all):
    d, tm = lay.d, lay.tm
    q_lora = wq_a_all.shape[2]
    wq_a = (wq_a_all, slot)
    heads, qk = wq_b.shape[1], wq_b.shape[2]
    kv_lora = kv_norm.shape[0]
    rope = wkv_a.shape[1] - kv_lora
    nope = qk - rope
    v_dim = wkv_b.shape[2] - nope
    assert nope == LANES and v_dim == LANES and rope <= LANES and rope % 4 == 0
    cos_t, sin_t = _mla_rope_tables(lay, rope)

    cq = _linear(h, wq_a, tm=tm, tn=min(512, q_lora), dtype=F32, name="mla_q_a")
    wkv_a_pad = jnp.pad(wkv_a, ((0, 0), (0, LANES - rope)))
    kv = _linear(h, wkv_a_pad, tm=tm, tn=kv_lora + LANES, dtype=F32, name="mla_kv_a")
    cqn, ckv, kr = _mla_norm(lay, cq, kv, q_norm, kv_norm, cos_t, sin_t, kv_lora, rope)

    hpt = 2
    wq = jnp.pad(wq_b, ((0, 0), (0, 0), (0, 2 * LANES - qk))).reshape(q_lora, heads * 2 * LANES)
    tn_q = hpt * 2 * LANES
    q = _fused_matmul(
        cqn, [_w2d(wq, tn_q)],
        [(cos_t, pl.BlockSpec((tm, LANES), lambda j, i: (i, 0))), (sin_t, pl.BlockSpec((tm, LANES), lambda j, i: (i, 0)))],
        [(jax.ShapeDtypeStruct((lay.n_tok, heads * 2 * LANES), BF16), _tile_spec(tm, tn_q))],
        functools.partial(_ep_mla_q, heads_per_tile=hpt, rope=rope),
        tm=tm, tn=tn_q, n_col_blocks=heads // hpt, name="mla_q_b")[0]

    wk = wkv_b[:, :, :nope].reshape(kv_lora, heads * nope)
    wv = wkv_b[:, :, nope:].reshape(kv_lora, heads * v_dim)
    hpt_k = 4
    k = _fused_matmul(
        ckv, [_w2d(wk, hpt_k * LANES)],
        [(kr, pl.BlockSpec((tm, LANES), lambda j, i: (i, 0)))],
        [(jax.ShapeDtypeStruct((lay.n_tok, heads * 2 * LANES), BF16), _tile_spec(tm, hpt_k * 2 * LANES))],
        functools.partial(_ep_mla_k, heads_per_tile=hpt_k),
        tm=tm, tn=hpt_k * LANES, n_col_blocks=heads // hpt_k, name="mla_k_b")[0]
    v = _linear(ckv, wv, tm=tm, tn=min(512, heads * v_dim), dtype=BF16, name="mla_v_b")

    o = _mla_attention(lay, q, k, v, heads, qk ** -0.5, tq=min(512, lay.seq), tk=min(512, lay.seq))
    wo = (wo_all.reshape(wo_all.shape[0], heads * v_dim, d), slot)
    return _residual_linear(lay, o, wo, tok, mods, layer, 2, tn=min(512, d), name="mla_o")


def _ep_rotate(accs, e_refs, o_refs, *, heads_per_tile, dk, scale):
    cos_ref, sin_ref = e_refs
    cos, sin = cos_ref[...], sin_ref[...]
    acc = accs[0]
    half = dk // 2
    for h in range(heads_per_tile):
        x1 = acc[:, h * dk:h * dk + half]
        x2 = acc[:, h * dk + half:(h + 1) * dk]
        o_refs[0][:, h * dk:h * dk + half] = ((x1 * cos - x2 * sin) * scale).astype(BF16)
        o_refs[0][:, h * dk + half:(h + 1) * dk] = ((x1 * sin + x2 * cos) * scale).astype(BF16)


def _rotated_linear(lay, a, w, cos, sin, dk, scale, name):
    tm = lay.tm
    n = w.shape[1]
    tn = 2 * dk
    half = dk // 2
    return _fused_matmul(
        a, [_w2d(w, tn)],
        [(cos, pl.BlockSpec((tm, half), lambda j, i: (i, 0))), (sin, pl.BlockSpec((tm, half), lambda j, i: (i, 0)))],
        [(jax.ShapeDtypeStruct((lay.n_tok, n), BF16), _tile_spec(tm, tn))],
        functools.partial(_ep_rotate, heads_per_tile=tn // dk, dk=dk, scale=scale),
        tm=tm, tn=tn, n_col_blocks=n // tn, name=name)[0]


def _retention_kernel(lg_ref, ql_ref, kl_ref, vl_ref, qc_ref, kc_ref, vc_ref, yl_ref, yc_ref,
                      sf_ref, sb_ref, yfl_ref, yfc_ref, ybl_ref, ybc_ref, *, chunk, heads):
    h = pl.program_id(1)
    lg_f = lg_ref[h]
    lg_b = lg_ref[heads + h]
    n_lat = ql_ref.shape[0] // chunk
    n_ctx = qc_ref.shape[0] // chunk
    dv = vl_ref.shape[1]

    row = lax.broadcasted_iota(jnp.int32, (chunk, chunk), 0).astype(F32)
    col = lax.broadcasted_iota(jnp.int32, (chunk, chunk), 1).astype(F32)
    pos = lax.broadcasted_iota(jnp.int32, (chunk, dv), 0).astype(F32)
    diff = row - col
    dmat_f = jnp.where(diff >= 0, jnp.exp(jnp.where(diff >= 0, diff, 0.0) * lg_f), 0.0)
    dmat_b = jnp.where(diff <= 0, jnp.exp(jnp.where(diff <= 0, -diff, 0.0) * lg_b), 0.0)
    qdec_f = jnp.exp((pos + 1.0) * lg_f)
    kdec_f = jnp.exp((chunk - 1.0 - pos) * lg_f)
    qdec_b = jnp.exp((chunk - pos) * lg_b)
    kdec_b = jnp.exp(pos * lg_b)
    blk_f = jnp.exp(chunk * lg_f)
    blk_b = jnp.exp(chunk * lg_b)

    def step(q, k, v, s_ref, dmat, qdec, kdec, blk):
        scores = lax.dot_general(q, k, (((1,), (1,)), ((), ())), preferred_element_type=F32) * dmat
        inner = jnp.dot(scores.astype(BF16), v, preferred_element_type=F32)
        state = s_ref[...]
        cross = jnp.dot(q, state.astype(BF16), preferred_element_type=F32) * qdec
        kd = (k.astype(F32) * kdec).astype(BF16)
        s_ref[...] = state * blk + lax.dot_general(kd, v, (((0,), (0,)), ((), ())), preferred_element_type=F32)
        return inner + cross

    def fwd(q_ref, k_ref, v_ref, y_ref, t):
        sl = pl.ds(pl.multiple_of(t * chunk, chunk), chunk)
        y_ref[sl, :] = step(q_ref[sl, :], k_ref[sl, :], v_ref[sl, :], sf_ref, dmat_f, qdec_f, kdec_f, blk_f)

    def bwd(q_ref, k_ref, v_ref, y_ref, t):
        sl = pl.ds(pl.multiple_of(t * chunk, chunk), chunk)
        y_ref[sl, :] = step(q_ref[sl, :], k_ref[sl, :], v_ref[sl, :], sb_ref, dmat_b, qdec_b, kdec_b, blk_b)

    sf_ref[...] = jnp.zeros_like(sf_ref)
    sb_ref[...] = jnp.zeros_like(sb_ref)
    for t in range(n_ctx):
        fwd(qc_ref, kc_ref, vc_ref, yfc_ref, t)
        bwd(qc_ref, kc_ref, vc_ref, ybc_ref, n_ctx - 1 - t)

    def body(t, carry):
        fwd(ql_ref, kl_ref, vl_ref, yfl_ref, t)
        bwd(ql_ref, kl_ref, vl_ref, ybl_ref, n_lat - 1 - t)
        return carry

    lax.fori_loop(0, n_lat, body, 0)

    def group_norm(y):
        mu = jnp.mean(y, axis=-1, keepdims=True)
        var = jnp.mean(jnp.square(y - mu), axis=-1, keepdims=True)
        return (y - mu) * lax.rsqrt(var + GROUP_NORM_EPS)

    for t in range(n_ctx):
        sl = pl.ds(t * chunk, chunk)
        yc_ref[sl, :] = group_norm(yfc_ref[sl, :] + ybc_ref[sl, :]).astype(BF16)

    def norm_body(t, carry):
        sl = pl.ds(pl.multiple_of(t * chunk, chunk), chunk)
        yl_ref[sl, :] = group_norm(yfl_ref[sl, :] + ybl_ref[sl, :]).astype(BF16)
        return carry

    lax.fori_loop(0, n_lat, norm_body, 0)


def _retention(lay, q, k, v, log_g, heads, dk, dv, chunk):
    b, s, c = lay.batch, lay.seq, lay.ctx
    ctx_blk0 = lay.n_lat // c
    lat = lambda w: pl.BlockSpec((s, w), lambda bi, h, lg: (bi, h))
    ctx = lambda w: pl.BlockSpec((c, w), lambda bi, h, lg: (ctx_blk0 + bi, h))
    y_lat, y_ctx = pl.pallas_call(
        functools.partial(_retention_kernel, chunk=chunk, heads=heads),
        grid_spec=pltpu.PrefetchScalarGridSpec(
            num_scalar_prefetch=1,
            grid=(b, heads),
            in_specs=[lat(dk), lat(dk), lat(dv), ctx(dk), ctx(dk), ctx(dv)],
            out_specs=[lat(dv), pl.BlockSpec((c, dv), lambda bi, h, lg: (bi, h))],
            scratch_shapes=[pltpu.VMEM((dk, dv), F32), pltpu.VMEM((dk, dv), F32),
                            pltpu.VMEM((s, dv), F32), pltpu.VMEM((c, dv), F32),
                            pltpu.VMEM((s, dv), F32), pltpu.VMEM((c, dv), F32)]),
        out_shape=[jax.ShapeDtypeStruct((lay.n_lat, heads * dv), BF16),
                   jax.ShapeDtypeStruct((b * c, heads * dv), BF16)],
        compiler_params=_params(2),
        name="retention",
    )(log_g.reshape(-1), q, k, v, q, k, v)
    return y_lat, y_ctx


def _ep_gate_mul(accs, e_refs, o_refs, *, lat_tiles):
    y_lat_ref, y_ctx_ref = e_refs
    y = jnp.where(pl.program_id(1) < lat_tiles, y_lat_ref[...], y_ctx_ref[...])
    o_refs[0][...] = (_silu(accs[0]) * y.astype(F32)).astype(BF16)


def _retention_mixer(lay, h, tok, mods, layer, wq, wk, wv, wg, wo, decay):
    d, tm = lay.d, lay.tm
    heads, dk = wq.shape[1], wq.shape[2]
    dv = wv.shape[2]
    assert dk == dv and dk % (2 * LANES) == 0
    chunk = min(256, lay.ctx)
    assert lay.ctx % chunk == 0 and lay.seq % chunk == 0
    log_g = -jnp.exp(decay.astype(F32))
    cos, sin = _ret_tables(lay, dk)
    q = _rotated_linear(lay, h, wq.reshape(d, heads * dk), cos, sin, dk, 1.0, "ret_q")
    k = _rotated_linear(lay, h, wk.reshape(d, heads * dk), cos, sin, dk, dk ** -0.5, "ret_k")
    v = _linear(h, wv.reshape(d, heads * dv), tm=tm, tn=min(512, heads * dv), dtype=BF16, name="ret_v")
    y_lat, y_ctx = _retention(lay, q, k, v, log_g, heads, dk, dv, chunk)
    tn = min(512, heads * dv)
    lat_tiles = lay.lat_tiles
    z = _fused_matmul(h, [_w2d(wg, tn)],
                      [(y_lat, pl.BlockSpec((tm, tn), lambda j, i: (jnp.minimum(i, lat_tiles - 1), j))),
                       (y_ctx, pl.BlockSpec((tm, tn), lambda j, i: (0, j)))],
                      [(jax.ShapeDtypeStruct((lay.n_tok, heads * dv), BF16), _tile_spec(tm, tn))],
                      functools.partial(_ep_gate_mul, lat_tiles=lat_tiles),
                      tm=tm, tn=tn, n_col_blocks=heads * dv // tn, name="ret_gate")[0]
    return _residual_linear(lay, z, wo, tok, mods, layer, 2, tn=min(512, d), name="ret_o")


def _ep_glu(accs, e_refs, o_refs):
    ba_ref, bg_ref = e_refs
    o_refs[0][...] = (accs[0] + ba_ref[...]) * jax.nn.sigmoid(accs[1] + bg_ref[...])


HALO = 16


SUBLANES = 8


def _dwconv_kernel(prev_ref, cur_ref, next_ref, w_ref, wb_ref, g_ref, b_ref, o_ref, buf_ref, acc_ref, *,
                   width, tt, tiles_per_seq, tiles_per_ctx, lat_tiles):
    i = pl.program_id(0)
    in_lat = i < lat_tiles
    pos = jnp.where(in_lat, i % tiles_per_seq, (i - lat_tiles) % tiles_per_ctx)
    n_seq_tiles = jnp.where(in_lat, tiles_per_seq, tiles_per_ctx)
    first = pos == 0
    last = pos == n_seq_tiles - 1
    buf_ref[0:HALO, :] = jnp.where(first, 0.0, prev_ref[...])
    buf_ref[HALO:HALO + tt, :] = cur_ref[...]
    buf_ref[HALO + tt:HALO + tt + HALO, :] = jnp.where(last, 0.0, next_ref[...])
    lead = HALO - width // 2
    n_out = tt // SUBLANES
    max_dblk = (lead + width - 1) // SUBLANES
    assert (n_out + max_dblk + 1) * SUBLANES <= tt + 2 * HALO
    sub = lax.broadcasted_iota(jnp.int32, (SUBLANES, LANES), 0)

    def lane_chunk(cidx, carry):
        ls = pl.ds(pl.multiple_of(cidx * LANES, LANES), LANES)
        bias = jnp.broadcast_to(wb_ref[:, ls], (SUBLANES, LANES))
        taps = [jnp.broadcast_to(w_ref[k:k + 1, ls], (SUBLANES, LANES)) for k in range(width)]

        def load(blk):
            v = buf_ref[blk * SUBLANES:(blk + 1) * SUBLANES, ls]
            return v, {s: pltpu.roll(v, SUBLANES - s, 0) for s in range(1, SUBLANES)}

        accs = {}
        nxt = load(0)
        for blk in range(n_out + max_dblk):
            (va, rolls_a), nxt = nxt, load(blk + 1)
            for s in range(SUBLANES):
                x = va if s == 0 else jnp.where(sub < SUBLANES - s, rolls_a[s], nxt[1][s])
                for dblk in range(max_dblk + 1):
                    k = SUBLANES * dblk + s - lead
                    out_blk = blk - dblk
                    if 0 <= k < width and 0 <= out_blk < n_out:
                        accs[out_blk] = accs.get(out_blk, bias) + x * taps[k]
            done = blk - max_dblk
            if done >= 0:
                acc_ref[done * SUBLANES:(done + 1) * SUBLANES, ls] = accs.pop(done)
        assert not accs
        return carry

    lax.fori_loop(0, cur_ref.shape[1] // LANES, lane_chunk, 0)

    ln_rows = 2 * SUBLANES

    def ln_block(rb, carry):
        rs = pl.ds(pl.multiple_of(rb * ln_rows, ln_rows), ln_rows)
        acc = acc_ref[rs, :]
        mu = jnp.mean(acc, axis=-1, keepdims=True)
        cen = acc - mu
        var = jnp.mean(cen * cen, axis=-1, keepdims=True)
        y = cen * lax.rsqrt(var + NORM_EPS) * g_ref[...] + b_ref[...]
        o_ref[rs, :] = _silu(y).astype(BF16)
        return carry

    lax.fori_loop(0, tt // ln_rows, ln_block, 0, unroll=4)


def _dwconv_ln_swish(lay, u, dw, dw_b, ln_g, ln_b):
    d = lay.d
    width = dw.shape[0]
    assert width // 2 <= HALO
    tt = min(256, lay.ctx)
    assert lay.ctx % tt == 0 and lay.seq % tt == 0 and tt % HALO == 0
    r = tt // HALO
    n_tiles = lay.n_tok // tt
    n_halo_blocks = lay.n_tok // HALO
    const = lambda rows: pl.BlockSpec((rows, d), lambda i: (0, 0))
    kern = functools.partial(_dwconv_kernel, width=width, tt=tt, tiles_per_seq=lay.seq // tt,
                             tiles_per_ctx=lay.ctx // tt, lat_tiles=lay.n_lat // tt)
    return pl.pallas_call(
        kern,
        grid=(n_tiles,),
        in_specs=[pl.BlockSpec((HALO, d), lambda i: (jnp.maximum(i * r - 1, 0), 0)),
                  pl.BlockSpec((tt, d), lambda i: (i, 0)),
                  pl.BlockSpec((HALO, d), lambda i: (jnp.minimum((i + 1) * r, n_halo_blocks - 1), 0)),
                  const(width), const(1), const(1), const(1)],
        out_specs=pl.BlockSpec((tt, d), lambda i: (i, 0)),
        out_shape=jax.ShapeDtypeStruct((lay.n_tok, d), BF16),
        scratch_shapes=[pltpu.VMEM((tt + 2 * HALO, d), F32), pltpu.VMEM((tt, d), F32)],
        compiler_params=_params(1),
        name="dwconv_ln_swish",
    )(u, u, u, dw, dw_b.reshape(1, d), ln_g.reshape(1, d), ln_b.reshape(1, d))


def _conformer_mixer(lay, h, tok, mods, layer, pw1, b1, dw, dw_b, ln_g, ln_b, pw2, b2):
    d, tm = lay.d, lay.tm
    tn = min(256, d)
    cols = d // tn
    b1r = b1.reshape(1, 2 * d)
    u = _fused_matmul(
        h,
        [(pw1, pl.BlockSpec((d, tn), lambda j, i: (0, j))), (pw1, pl.BlockSpec((d, tn), lambda j, i: (0, cols + j)))],
        [(b1r, pl.BlockSpec((1, tn), lambda j, i: (0, j))), (b1r, pl.BlockSpec((1, tn), lambda j, i: (0, cols + j)))],
        [(jax.ShapeDtypeStruct((lay.n_tok, d), F32), _tile_spec(tm, tn))],
        _ep_glu, tm=tm, tn=tn, n_col_blocks=cols, name="conv_pw1_glu")[0]
    a = _dwconv_ln_swish(lay, u, dw, dw_b, ln_g, ln_b)
    return _residual_linear(lay, a, pw2, tok, mods, layer, 2, tn=min(512, d), bias=b2, name="conv_pw2")


MOE_ROW_TILE = 256


def _dispatch_plan(idx, wts, n_exp, tmx):
    eid = idx[:, :MOE_TOP_K].reshape(-1)
    w = wts[:, :MOE_TOP_K].reshape(-1)
    n_pairs = eid.shape[0]
    onehot = (eid[:, None] == jnp.arange(n_exp, dtype=jnp.int32)[None, :]).astype(jnp.int32)
    csum = jnp.cumsum(onehot, axis=0)
    rank = jnp.sum((csum - onehot) * onehot, axis=1)
    counts = csum[-1]
    padded = ((counts + tmx - 1) // tmx) * tmx
    ends = jnp.cumsum(padded)
    starts = ends - padded
    pos = (starts[eid] + rank).astype(jnp.int32)
    n_tiles = (n_pairs + n_exp * (tmx - 1) + tmx - 1) // tmx
    n_slots = n_tiles * tmx
    n_used = (ends[-1] // tmx).astype(jnp.int32).reshape(1)
    tile_start = jnp.arange(n_tiles, dtype=jnp.int32) * tmx
    tile_expert = jnp.sum((tile_start[:, None] >= ends[None, :]).astype(jnp.int32), axis=1)
    tile_expert = jnp.minimum(tile_expert, n_exp - 1).astype(jnp.int32)
    pair_token = jnp.arange(n_pairs, dtype=jnp.int32) // MOE_TOP_K
    src_token = jnp.zeros((n_slots,), jnp.int32).at[pos].set(pair_token, unique_indices=True)
    slot_w = jnp.zeros((n_slots,), F32).at[pos].set(w, unique_indices=True)
    return pos, src_token, slot_w.reshape(n_slots, 1), tile_expert, n_used


def _row_gather(src_hbm, dst, sem, index_of_row, n_rows):
    def body(r, carry):
        pltpu.make_async_copy(src_hbm.at[pl.ds(index_of_row(r), 1), :], dst.at[pl.ds(r, 1), :], sem).start()
        return carry
    lax.fori_loop(0, n_rows, body, 0, unroll=8)


def _row_gather_wait(src_hbm, dst, sem):
    pltpu.make_async_copy(src_hbm.at[pl.ds(0, dst.shape[0]), :], dst, sem).wait()


def _moe_up_kernel(te_ref, src_ref, nu_ref, hp_hbm, w1_ref, w3_ref, sw_ref, act_ref, buf, sem, w1b, w3b, *, tmx):
    t = pl.program_id(0)
    n_used = nu_ref[0]

    def start_tile(tile, slot):
        _row_gather(hp_hbm, buf.at[slot], sem.at[slot], lambda r: src_ref[tile * tmx + r], tmx)

    @pl.when(t == 0)
    def _():
        start_tile(0, 0)

    @pl.when(t + 1 < n_used)
    def _():
        start_tile(t + 1, (t + 1) % 2)

    @pl.when(t < n_used)
    def _():
        slot = t % 2
        new_expert = jnp.logical_or(t == 0, te_ref[t] != te_ref[jnp.maximum(t - 1, 0)])

        @pl.when(new_expert)
        def _():
            w1b[...] = w1_ref[...].astype(BF16)
            w3b[...] = w3_ref[...].astype(BF16)

        _row_gather_wait(hp_hbm, buf.at[slot], sem.at[slot])
        a_lo, a_hi = _unpack_bf16_pairs(buf[slot])
        half = a_lo.shape[1]
        u1 = (jnp.dot(a_lo, w1b[:half, :], preferred_element_type=F32)
              + jnp.dot(a_hi, w1b[half:, :], preferred_element_type=F32))
        u3 = (jnp.dot(a_lo, w3b[:half, :], preferred_element_type=F32)
              + jnp.dot(a_hi, w3b[half:, :], preferred_element_type=F32))
        act_ref[...] = (_silu(u1) * u3 * sw_ref[...]).astype(BF16)

    @pl.when(t >= n_used)
    def _():
        act_ref[...] = jnp.zeros_like(act_ref)


def _moe_down_kernel(te_ref, nu_ref, act_ref, w2_ref, y_ref, w2b):
    t = pl.program_id(0)
    n_used = nu_ref[0]

    @pl.when(t < n_used)
    def _():
        new_expert = jnp.logical_or(t == 0, te_ref[t] != te_ref[jnp.maximum(t - 1, 0)])

        @pl.when(new_expert)
        def _():
            w2b[...] = w2_ref[...].astype(BF16)

        y_ref[...] = jnp.dot(act_ref[...], w2b[...], preferred_element_type=F32)

    @pl.when(t >= n_used)
    def _():
        y_ref[...] = jnp.zeros_like(y_ref)


def _moe_combine_kernel(pos_ref, y_hbm, tok_ref, gate_ref, o_ref, buf, sem, *, tmc, n_steps):
    i = pl.program_id(0)

    def start_tile(tile, slot):
        def index_of_row(r):
            choice = r // tmc
            return pos_ref[(tile * tmc + (r - choice * tmc)) * MOE_TOP_K + choice]
        _row_gather(y_hbm, buf.at[slot], sem.at[slot], index_of_row, MOE_TOP_K * tmc)

    @pl.when(i == 0)
    def _():
        start_tile(0, 0)

    @pl.when(i + 1 < n_steps)
    def _():
        start_tile(i + 1, (i + 1) % 2)

    slot = i % 2
    _row_gather_wait(y_hbm, buf.at[slot], sem.at[slot])
    o_ref[...] = tok_ref[...] + gate_ref[...] * (buf[slot, 0:tmc, :] + buf[slot, tmc:2 * tmc, :])


def _moe_routed(lay, hp, idx, wts, tok, mods, layer, w1, w3, w2):
    d = lay.d
    n_exp, d_ff = w1.shape[1], w1.shape[3]
    tmx = MOE_ROW_TILE
    tmc = lay.tm // 2
    assert MOE_TOP_K == 2 and lay.n_tok % tmc == 0
    pos, src_token, slot_w, tile_expert, n_used = _dispatch_plan(idx, wts, n_exp, tmx)
    n_slots = src_token.shape[0]
    n_tiles = n_slots // tmx

    act = pl.pallas_call(
        functools.partial(_moe_up_kernel, tmx=tmx),
        grid_spec=pltpu.PrefetchScalarGridSpec(
            num_scalar_prefetch=3,
            grid=(n_tiles,),
            in_specs=[pl.BlockSpec(memory_space=pl.ANY),
                      pl.BlockSpec((None, None, d, d_ff), lambda t, te, src, nu: (layer, te[t], 0, 0)),
                      pl.BlockSpec((None, None, d, d_ff), lambda t, te, src, nu: (layer, te[t], 0, 0)),
                      pl.BlockSpec((tmx, 1), lambda t, te, src, nu: (t, 0))],
            out_specs=pl.BlockSpec((tmx, d_ff), lambda t, te, src, nu: (t, 0)),
            scratch_shapes=[pltpu.VMEM((2, tmx, d // 2), jnp.uint32), pltpu.SemaphoreType.DMA((2,)),
                            pltpu.VMEM((d, d_ff), BF16), pltpu.VMEM((d, d_ff), BF16)]),
        out_shape=jax.ShapeDtypeStruct((n_slots, d_ff), BF16),
        compiler_params=_params(1),
        name="moe_up",
    )(tile_expert, src_token, n_used, hp, w1, w3, slot_w)

    y = pl.pallas_call(
        _moe_down_kernel,
        grid_spec=pltpu.PrefetchScalarGridSpec(
            num_scalar_prefetch=2,
            grid=(n_tiles,),
            in_specs=[pl.BlockSpec((tmx, d_ff), lambda t, te, nu: (t, 0)),
                      pl.BlockSpec((None, None, d_ff, d), lambda t, te, nu: (layer, te[t], 0, 0))],
            out_specs=pl.BlockSpec((tmx, d), lambda t, te, nu: (t, 0)),
            scratch_shapes=[pltpu.VMEM((d_ff, d), BF16)]),
        out_shape=jax.ShapeDtypeStruct((n_slots, d), F32),
        compiler_params=_params(1),
        name="moe_down",
    )(tile_expert, n_used, act, w2)

    n_steps = lay.n_tok // tmc
    gate_rows = lay.tm // tmc
    return pl.pallas_call(
        functools.partial(_moe_combine_kernel, tmc=tmc, n_steps=n_steps),
        grid_spec=pltpu.PrefetchScalarGridSpec(
            num_scalar_prefetch=1,
            grid=(n_steps,),
            in_specs=[pl.BlockSpec(memory_space=pl.ANY),
                      pl.BlockSpec((tmc, d), lambda i, pos: (i, 0)),
                      pl.BlockSpec((None, 1, d), lambda i, pos: (_mod_row(lay, layer, i // gate_rows), 0, 5))],
            out_specs=pl.BlockSpec((tmc, d), lambda i, pos: (i, 0)),
            scratch_shapes=[pltpu.VMEM((2, MOE_TOP_K * tmc, d), F32), pltpu.SemaphoreType.DMA((2,))]),
        out_shape=jax.ShapeDtypeStruct((lay.n_tok, d), F32),
        compiler_params=_params(1),
        name="moe_combine",
    )(pos, y, tok, mods)


def kernel(x, c, ctx, c_ctx, ada_w, ada_b, norm_mix, norm_ffn, mla_wq_a, mla_q_norm, mla_wq_b, mla_wkv_a, mla_kv_norm, mla_wkv_b, mla_wo, ret_wq, ret_wk, ret_wv, ret_wg, ret_wo, ret_decay, conv_pw1, conv_b1, conv_dw, conv_dw_b, conv_ln_g, conv_ln_b, conv_pw2, conv_b2, moe_wg_router, moe_bg_router, moe_we_router, moe_be_router, moe_w1, moe_w3, moe_w2, final_norm):
    b, s, d = x.shape
    n_ctx = ctx.shape[1]
    depth = ada_w.shape[0]
    n_mixers = 3
    lay = Layout(batch=b, seq=s, ctx=n_ctx, d=d, tm=b * n_ctx)
    assert s % lay.tm == 0 and b + 1 <= MOD_ROWS and s % GRID_W == 0

    tok = jnp.concatenate([x.reshape(b * s, d), ctx.reshape(b * n_ctx, d)], axis=0)
    cond = jnp.zeros((MOD_ROWS, d), F32).at[:b].set(c).at[b].set(c_ctx)
    mods = _adaln(cond, ada_w, ada_b, tn=min(1024, d)).reshape(depth * MOD_ROWS, 1, N_MOD * d)

    for i in range(depth):
        kind, slot = i % n_mixers, i // n_mixers
        h = _normmod(lay, tok, norm_mix[i], mods, i, 0, 1)
        if kind == 0:
            tok = _mla_mixer(lay, h, tok, mods, i, slot, mla_wq_a, mla_q_norm[slot], mla_wq_b[slot], mla_wkv_a[slot],
                             mla_kv_norm[slot], mla_wkv_b[slot], mla_wo)
        elif kind == 1:
            tok = _retention_mixer(lay, h, tok, mods, i, ret_wq[slot], ret_wk[slot], ret_wv[slot], ret_wg[slot],
                                   ret_wo[slot], ret_decay[slot])
        else:
            tok = _conformer_mixer(lay, h, tok, mods, i, conv_pw1[slot], conv_b1[slot], conv_dw[slot], conv_dw_b[slot],
                                   conv_ln_g[slot], conv_ln_b[slot], conv_pw2[slot], conv_b2[slot])
        hp, idx, wts = _normmod_router(lay, tok, norm_ffn[i], mods, i, 3, 4, moe_wg_router[i], moe_bg_router[i],
                                       moe_we_router[i], moe_be_router[i])
        tok = _moe_routed(lay, hp, idx, wts, tok, mods, i, moe_w1, moe_w3, moe_w2)
    return _final_norm(lay, tok, final_norm).reshape(b, s, d)
```

```python
import functools
import math
from typing import NamedTuple

import jax
import jax.numpy as jnp
from jax import lax
from jax.experimental import pallas as pl
from jax.experimental.pallas import tpu as pltpu

F32 = jnp.float32
BF16 = jnp.bfloat16

GRID_W = 64
ROPE_BASE = 10000.0
RET_THETA_BASE = 10000.0
NORM_EPS = 1e-6
GROUP_NORM_EPS = 1e-5
N_MOD = 6
MOE_TOP_K = 2

LANES = 128
MOD_ROWS = 8
VMEM_LIMIT_BYTES = 56 * 1024 * 1024
LOG2E = math.log2(math.e)


class Layout(NamedTuple):
    batch: int
    seq: int
    ctx: int
    d: int
    tm: int

    @property
    def n_lat(self):
        return self.batch * self.seq

    @property
    def n_tok(self):
        return self.batch * (self.seq + self.ctx)

    @property
    def lat_tiles(self):
        return self.n_lat // self.tm

    @property
    def row_tiles(self):
        return self.n_tok // self.tm

    @property
    def tiles_per_batch(self):
        return self.seq // self.tm


def _params(n_axes):
    return pltpu.CompilerParams(dimension_semantics=("arbitrary",) * n_axes,
                                vmem_limit_bytes=VMEM_LIMIT_BYTES)


def _mod_row(lay, layer, i):
    return layer * MOD_ROWS + jnp.minimum(i // lay.tiles_per_batch, lay.batch)


def _silu(v):
    return v * jax.nn.sigmoid(v)


def _adaln_kernel(c_ref, w_ref, b_ref, o_ref):
    @pl.when(pl.program_id(1) == 0)
    def _():
        o_ref[...] = jnp.broadcast_to(b_ref[...], o_ref.shape)

    a = _silu(c_ref[...]).astype(BF16)
    o_ref[...] += jnp.dot(a, w_ref[...].astype(BF16), preferred_element_type=F32)


def _adaln(cond, ada_w, ada_b, tk):
    n_layers, d, n = ada_w.shape
    return pl.pallas_call(
        _adaln_kernel,
        grid=(n_layers, d // tk),
        in_specs=[pl.BlockSpec((MOD_ROWS, tk), lambda l, k: (0, k)),
                  pl.BlockSpec((None, tk, n), lambda l, k: (l, k, 0)),
                  pl.BlockSpec((None, 1, n), lambda l, k: (l, 0, 0))],
        out_specs=pl.BlockSpec((None, MOD_ROWS, n), lambda l, k: (l, 0, 0)),
        out_shape=jax.ShapeDtypeStruct((n_layers, MOD_ROWS, n), F32),
        compiler_params=_params(2),
        name="adaln",
    )(cond, ada_w, ada_b.reshape(n_layers, 1, n))


def _rms(x, g):
    return x * lax.rsqrt(jnp.mean(x * x, axis=-1, keepdims=True) + NORM_EPS) * g


def _normmod_kernel(t_ref, g_ref, sh_ref, sc_ref, h_ref):
    h = _rms(t_ref[...], g_ref[...]) * (1.0 + sc_ref[...]) + sh_ref[...]
    h_ref[...] = h.astype(BF16)


def _route(logits, n_groups, per_group):
    n_exp = n_groups * per_group
    lane = lax.broadcasted_iota(jnp.int32, logits.shape, 1)
    neg = jnp.float32(-jnp.inf)
    is_g = (lane >= n_exp) & (lane < n_exp + n_groups)
    lg = jnp.where(is_g, logits, neg)
    mg = jnp.max(lg, axis=1, keepdims=True)
    g_sel = jnp.min(jnp.where(lg == mg, lane, LANES), axis=1, keepdims=True) - n_exp
    pg_sel = 1.0 / jnp.sum(jnp.where(is_g, jnp.exp(lg - mg), 0.0), axis=1, keepdims=True)
    in_sel = (lane >= g_sel * per_group) & (lane < (g_sel + 1) * per_group)
    le = jnp.where(in_sel, logits, neg)
    m1 = jnp.max(le, axis=1, keepdims=True)
    i1 = jnp.min(jnp.where(le == m1, lane, LANES), axis=1, keepdims=True)
    le2 = jnp.where(lane == i1, neg, le)
    m2 = jnp.max(le2, axis=1, keepdims=True)
    i2 = jnp.min(jnp.where(le2 == m2, lane, LANES), axis=1, keepdims=True)
    e2 = jnp.exp(m2 - m1)
    w1 = pg_sel / (1.0 + e2)
    w2 = pg_sel * e2 / (1.0 + e2)
    idx = jnp.where(lane == 0, i1, jnp.where(lane == 1, i2, 0))
    wts = jnp.where(lane == 0, w1, jnp.where(lane == 1, w2, 0.0))
    return idx, wts


def _pack_bf16_pairs(h):
    half = h.shape[1] // 2
    hb = h.astype(BF16).astype(F32)
    lo = lax.shift_right_logical(lax.bitcast_convert_type(hb[:, :half], jnp.uint32), jnp.uint32(16))
    hi = lax.bitcast_convert_type(hb[:, half:], jnp.uint32) & jnp.uint32(0xFFFF0000)
    return hi | lo


def _unpack_bf16_pairs(w):
    lo = lax.bitcast_convert_type(lax.shift_left(w, jnp.uint32(16)), F32)
    hi = lax.bitcast_convert_type(w & jnp.uint32(0xFFFF0000), F32)
    return lo.astype(BF16), hi.astype(BF16)


def _normmod_router_kernel(t_ref, g_ref, sh_ref, sc_ref, w2_ref, wh_ref, rb_ref, hp_ref, idx_ref, wts_ref, *,
                           n_groups, per_group):
    h = _rms(t_ref[...], g_ref[...]) * (1.0 + sc_ref[...]) + sh_ref[...]
    hp_ref[...] = _pack_bf16_pairs(h)
    h_hi = h.astype(BF16)
    h_lo = (h - h_hi.astype(F32)).astype(BF16)
    hh = jnp.dot(h_hi, w2_ref[...], preferred_element_type=F32)
    hl = jnp.dot(h_lo, wh_ref[...], preferred_element_type=F32)
    logits = hh[:, :LANES] + hh[:, LANES:] + hl + rb_ref[...]
    idx, wts = _route(logits, n_groups, per_group)
    idx_ref[...] = idx
    wts_ref[...] = wts


def _mod_specs(lay, layer, shift_idx, scale_idx):
    d = lay.d
    return [pl.BlockSpec((None, 1, d), lambda i: (_mod_row(lay, layer, i), 0, shift_idx)),
            pl.BlockSpec((None, 1, d), lambda i: (_mod_row(lay, layer, i), 0, scale_idx))]


def _normmod(lay, tok, g, mods, layer, shift_idx, scale_idx):
    d, tm = lay.d, lay.tm
    return pl.pallas_call(
        _normmod_kernel,
        grid=(lay.row_tiles,),
        in_specs=[pl.BlockSpec((tm, d), lambda i: (i, 0)),
                  pl.BlockSpec((1, d), lambda i: (0, 0))] + _mod_specs(lay, layer, shift_idx, scale_idx),
        out_specs=pl.BlockSpec((tm, d), lambda i: (i, 0)),
        out_shape=jax.ShapeDtypeStruct((lay.n_tok, d), BF16),
        compiler_params=_params(1),
        name="normmod",
    )(tok, g.reshape(1, d), mods, mods)


def _normmod_router(lay, tok, g, mods, layer, shift_idx, scale_idx, wg_r, bg_r, we_r, be_r):
    d, tm = lay.d, lay.tm
    n_groups, per_group = we_r.shape[1], we_r.shape[2]
    n_exp = n_groups * per_group
    assert n_exp + n_groups <= LANES
    wr = jnp.concatenate([we_r.reshape(d, n_exp), wg_r], axis=1)
    wr = jnp.pad(wr, ((0, 0), (0, LANES - wr.shape[1])))
    rb = jnp.pad(jnp.concatenate([be_r.reshape(n_exp), bg_r]), (0, LANES - n_exp - n_groups)).reshape(1, LANES)
    w_hi = wr.astype(BF16)
    w_lo = (wr - w_hi.astype(F32)).astype(BF16)
    w2 = jnp.concatenate([w_hi, w_lo], axis=1)
    return pl.pallas_call(
        functools.partial(_normmod_router_kernel, n_groups=n_groups, per_group=per_group),
        grid=(lay.row_tiles,),
        in_specs=[pl.BlockSpec((tm, d), lambda i: (i, 0)),
                  pl.BlockSpec((1, d), lambda i: (0, 0))] + _mod_specs(lay, layer, shift_idx, scale_idx) + [
                  pl.BlockSpec((d, 2 * LANES), lambda i: (0, 0)),
                  pl.BlockSpec((d, LANES), lambda i: (0, 0)),
                  pl.BlockSpec((1, LANES), lambda i: (0, 0))],
        out_specs=[pl.BlockSpec((tm, d // 2), lambda i: (i, 0)),
                   pl.BlockSpec((tm, LANES), lambda i: (i, 0)),
                   pl.BlockSpec((tm, LANES), lambda i: (i, 0))],
        out_shape=[jax.ShapeDtypeStruct((lay.n_tok, d // 2), jnp.uint32),
                   jax.ShapeDtypeStruct((lay.n_tok, LANES), jnp.int32),
                   jax.ShapeDtypeStruct((lay.n_tok, LANES), F32)],
        compiler_params=_params(1),
        name="normmod_router",
    )(tok, g.reshape(1, d), mods, mods, w2, w_hi, rb)


def _final_norm_kernel(t_ref, g_ref, o_ref):
    o_ref[...] = _rms(t_ref[...], g_ref[...])


def _final_norm(lay, tok, g):
    d, tm = lay.d, lay.tm
    return pl.pallas_call(
        _final_norm_kernel,
        grid=(lay.lat_tiles,),
        in_specs=[pl.BlockSpec((tm, d), lambda i: (i, 0)),
                  pl.BlockSpec((1, d), lambda i: (0, 0))],
        out_specs=pl.BlockSpec((tm, d), lambda i: (i, 0)),
        out_shape=jax.ShapeDtypeStruct((lay.n_lat, d), F32),
        compiler_params=_params(1),
        name="final_norm",
    )(tok, g.reshape(1, d))


def _fused_matmul_kernel(*refs, n_a, n_w, n_extra, n_out, epilogue, lat_tiles):
    a_refs = refs[:n_a]
    w_refs = refs[n_a:n_a + n_w]
    e_refs = refs[n_a + n_w:n_a + n_w + n_extra]
    o_refs = refs[n_a + n_w + n_extra:n_a + n_w + n_extra + n_out]
    wb_refs = refs[n_a + n_w + n_extra + n_out:]

    @pl.when(pl.program_id(1) == 0)
    def _():
        for w_ref, wb_ref in zip(w_refs, wb_refs):
            wb_ref[...] = w_ref[...].astype(BF16)

    def run(a_ref):
        a = a_ref[...]
        accs = [jnp.dot(a, wb_ref[...], preferred_element_type=F32) for wb_ref in wb_refs]
        epilogue(accs, e_refs, o_refs)

    if n_a == 1:
        run(a_refs[0])
    else:
        is_lat = pl.program_id(1) < lat_tiles
        pl.when(is_lat)(lambda: run(a_refs[0]))
        pl.when(jnp.logical_not(is_lat))(lambda: run(a_refs[1]))


def _fused_matmul(a, w_specs, extras, outs, epilogue, *, tm, tn, n_col_blocks, name):
    if isinstance(a, tuple):
        a_lat, a_ctx = a
        k = a_lat.shape[1]
        lat_tiles = a_lat.shape[0] // tm
        assert a_ctx.shape == (tm, k)
        a_args = [a_lat, a_ctx]
        a_specs = [pl.BlockSpec((tm, k), lambda j, i: (jnp.minimum(i, lat_tiles - 1), 0)),
                   pl.BlockSpec((tm, k), lambda j, i: (0, 0))]
        row_tiles = lat_tiles + 1
    else:
        m, k = a.shape
        lat_tiles = None
        a_args = [a]
        a_specs = [pl.BlockSpec((tm, k), lambda j, i: (i, 0))]
        row_tiles = m // tm
    kernel = functools.partial(_fused_matmul_kernel, n_a=len(a_args), n_w=len(w_specs), n_extra=len(extras),
                               n_out=len(outs), epilogue=epilogue, lat_tiles=lat_tiles)
    res = pl.pallas_call(
        kernel,
        grid=(n_col_blocks, row_tiles),
        in_specs=a_specs + [s for _, s in w_specs] + [s for _, s in extras],
        out_specs=[s for _, s in outs],
        out_shape=[s for s, _ in outs],
        scratch_shapes=[pltpu.VMEM((k, tn), BF16) for _ in w_specs],
        compiler_params=_params(2),
        name=name,
    )(*a_args, *[w for w, _ in w_specs], *[e for e, _ in extras])
    return res


def _w2d(w, tn):
    if isinstance(w, tuple):
        stack, slot = w
        return (stack, pl.BlockSpec((None, stack.shape[1], tn), lambda j, i: (slot, 0, j)))
    return (w, pl.BlockSpec((w.shape[0], tn), lambda j, i: (0, j)))


def _tile_spec(tm, tn):
    return pl.BlockSpec((tm, tn), lambda j, i: (i, j))


def _row_spec(tn):
    return pl.BlockSpec((1, tn), lambda j, i: (0, j))


def _ep_store(dtype):
    def ep(accs, e_refs, o_refs):
        o_refs[0][...] = accs[0].astype(dtype)
    return ep


def _linear(a, w, *, tm, tn, dtype, name):
    n = w[0].shape[2] if isinstance(w, tuple) else w.shape[1]
    return _fused_matmul(a, [_w2d(w, tn)], [], [(jax.ShapeDtypeStruct((a.shape[0], n), dtype), _tile_spec(tm, tn))],
                         _ep_store(dtype), tm=tm, tn=tn, n_col_blocks=n // tn, name=name)[0]


def _ep_residual(accs, e_refs, o_refs):
    tok_ref, gate_ref = e_refs
    o_refs[0][...] = tok_ref[...] + gate_ref[...] * accs[0]


def _ep_residual_bias(accs, e_refs, o_refs):
    tok_ref, gate_ref, b_ref = e_refs
    o_refs[0][...] = tok_ref[...] + gate_ref[...] * (accs[0] + b_ref[...])


def _residual_linear(lay, a, w, tok, mods, layer, gate_idx, *, tn, bias=None, name):
    d, tm = lay.d, lay.tm
    cols = d // tn
    extras = [(tok, _tile_spec(tm, tn)),
              (mods, pl.BlockSpec((None, 1, tn), lambda j, i: (_mod_row(lay, layer, i), 0, gate_idx * cols + j)))]
    ep = _ep_residual
    if bias is not None:
        extras.append((bias.reshape(1, d), _row_spec(tn)))
        ep = _ep_residual_bias
    return _fused_matmul(a, [_w2d(w, tn)], extras, [(jax.ShapeDtypeStruct((lay.n_tok, d), F32), _tile_spec(tm, tn))],
                         ep, tm=tm, tn=tn, n_col_blocks=cols, name=name)[0]


def _flat_positions_table(lay, lat_table, ctx_row):
    lat = jnp.tile(lat_table, (lay.batch, 1))
    ctx = jnp.broadcast_to(ctx_row, (lay.batch * lay.ctx, lat_table.shape[1]))
    return jnp.concatenate([lat, ctx], axis=0)


def _mla_rope_tables(lay, rope_dim):
    rows = lay.seq // GRID_W
    grid = jnp.stack(jnp.meshgrid(jnp.arange(rows), jnp.arange(GRID_W), indexing='ij'), axis=-1)
    grid = grid.reshape(-1, 2).astype(F32)
    n_freq = rope_dim // 4
    inv = ROPE_BASE ** (-jnp.arange(n_freq, dtype=F32) / n_freq)
    ang = jnp.concatenate([grid[:, :1] * inv, grid[:, 1:] * inv], axis=-1)
    cos, sin = jnp.cos(ang), jnp.sin(ang)
    pad = LANES - rope_dim
    cos_t = jnp.concatenate([cos, cos, jnp.ones((lay.seq, pad), F32)], axis=-1)
    sin_t = jnp.concatenate([-sin, sin, jnp.zeros((lay.seq, pad), F32)], axis=-1)
    one = jnp.ones((1, LANES), F32)
    return _flat_positions_table(lay, cos_t, one), _flat_positions_table(lay, sin_t, 0.0 * one)


def _rope_slab(r, cos_t, sin_t, half):
    lane = lax.broadcasted_iota(jnp.int32, r.shape, 1)
    partner = jnp.where(lane < half, pltpu.roll(r, LANES - half, 1), pltpu.roll(r, half, 1))
    return r * cos_t + partner * sin_t


def _ret_tables(lay, dk):
    theta = 1.0 / (RET_THETA_BASE ** jnp.linspace(0.0, 1.0, dk // 2, dtype=F32))
    ang = jnp.arange(lay.seq, dtype=F32)[:, None] * theta
    one = jnp.ones((1, dk // 2), F32)
    return (_flat_positions_table(lay, jnp.cos(ang), one),
            _flat_positions_table(lay, jnp.sin(ang), 0.0 * one))


def _mla_norm_kernel(cq_ref, kv_ref, qn_ref, kvn_ref, cos_ref, sin_ref, cqn_ref, ckv_ref, kr_ref, *, kv_lora, rope):
    cqn_ref[...] = _rms(cq_ref[...], qn_ref[...]).astype(BF16)
    kv = kv_ref[...]
    ckv_ref[...] = _rms(kv[:, :kv_lora], kvn_ref[...]).astype(BF16)
    kr_ref[...] = _rope_slab(kv[:, kv_lora:], cos_ref[...], sin_ref[...], rope // 2).astype(BF16)


def _mla_norm(lay, cq, kv, q_norm, kv_norm, cos_t, sin_t, kv_lora, rope):
    tm = lay.tm
    q_lora = cq.shape[1]
    row = lambda w: pl.BlockSpec((tm, w), lambda i: (i, 0))
    const = lambda w: pl.BlockSpec((1, w), lambda i: (0, 0))
    return pl.pallas_call(
        functools.partial(_mla_norm_kernel, kv_lora=kv_lora, rope=rope),
        grid=(lay.row_tiles,),
        in_specs=[row(q_lora), row(kv_lora + LANES), const(q_lora), const(kv_lora), row(LANES), row(LANES)],
        out_specs=[row(q_lora), row(kv_lora), row(LANES)],
        out_shape=[jax.ShapeDtypeStruct((lay.n_tok, q_lora), BF16),
                   jax.ShapeDtypeStruct((lay.n_tok, kv_lora), BF16),
                   jax.ShapeDtypeStruct((lay.n_tok, LANES), BF16)],
        compiler_params=_params(1),
        name="mla_norm",
    )(cq, kv, q_norm.reshape(1, q_lora), kv_norm.reshape(1, kv_lora), cos_t, sin_t)


def _ep_mla_q(accs, e_refs, o_refs, *, heads_per_tile, rope):
    cos_ref, sin_ref = e_refs
    acc = accs[0]
    cos_t, sin_t = cos_ref[...], sin_ref[...]
    for h in range(heads_per_tile):
        base = h * 2 * LANES
        o_refs[0][:, base:base + LANES] = acc[:, base:base + LANES].astype(BF16)
        slab = _rope_slab(acc[:, base + LANES:base + 2 * LANES], cos_t, sin_t, rope // 2)
        o_refs[0][:, base + LANES:base + 2 * LANES] = slab.astype(BF16)


def _ep_mla_k(accs, e_refs, o_refs, *, heads_per_tile):
    kr = e_refs[0][...]
    acc = accs[0]
    for h in range(heads_per_tile):
        o_refs[0][:, h * 2 * LANES:h * 2 * LANES + LANES] = acc[:, h * LANES:(h + 1) * LANES].astype(BF16)
        o_refs[0][:, h * 2 * LANES + LANES:(h + 1) * 2 * LANES] = kr


def _attn_scores(q, kv_refs, chunks, s_ref):
    off = 0
    for ref_idx, start, size in chunks:
        k = kv_refs[2 * ref_idx][start:start + size, :]
        s_ref[:, off:off + size] = lax.dot_general(q, k, (((1,), (1,)), ((), ())), preferred_element_type=F32)
        off += size


ATTN_ROW_BLOCK = 64


def _attn_probs(s_ref, p_ref, l_ref, c):
    tq, n_keys = s_ref.shape
    rb = min(ATTN_ROW_BLOCK, tq)
    for r0 in range(0, tq, rb):
        m_acc = s_ref[r0:r0 + rb, 0:LANES]
        for j in range(1, n_keys // LANES):
            m_acc = jnp.maximum(m_acc, s_ref[r0:r0 + rb, j * LANES:(j + 1) * LANES])
        mc = jnp.broadcast_to(jnp.max(m_acc, axis=1, keepdims=True) * c, (rb, LANES))
        l_acc = jnp.zeros((rb, LANES), F32)
        for j in range(n_keys // LANES):
            p = jnp.exp2(s_ref[r0:r0 + rb, j * LANES:(j + 1) * LANES] * c - mc)
            l_acc = l_acc + p
            p_ref[r0:r0 + rb, j * LANES:(j + 1) * LANES] = p.astype(BF16)
        l_ref[r0:r0 + rb, :] = jnp.broadcast_to(jnp.sum(l_acc, axis=1, keepdims=True), (rb, LANES))


def _attn_output(p_ref, l_ref, kv_refs, n_kv):
    acc = None
    off = 0
    for ref_idx in range(n_kv):
        v_ref = kv_refs[2 * ref_idx + 1]
        n = v_ref.shape[0]
        part = jnp.dot(p_ref[:, off:off + n], v_ref[...], preferred_element_type=F32)
        acc = part if acc is None else acc + part
        off += n
    return acc / l_ref[...]


def _attn_kernel(q_ref, *refs, chunks, n_kv, scale, tq):
    kv_refs = refs[:2 * n_kv]
    o_ref = refs[2 * n_kv]
    scratch = refs[2 * n_kv + 1:]
    s_refs, p_refs, l_refs = scratch[0:2], scratch[2:4], scratch[4:6]
    assert o_ref.shape[1] == LANES
    nq = q_ref.shape[0] // tq
    c = scale * LOG2E

    def rows(i):
        return slice(i * tq, (i + 1) * tq)

    _attn_scores(q_ref[rows(0), :], kv_refs, chunks, s_refs[0])
    for i in range(nq):
        cur, nxt = i % 2, (i + 1) % 2
        if i >= 1:
            o_ref[rows(i - 1), :] = _attn_output(p_refs[nxt], l_refs[nxt], kv_refs, n_kv).astype(o_ref.dtype)
        if i + 1 < nq:
            _attn_scores(q_ref[rows(i + 1), :], kv_refs, chunks, s_refs[nxt])
        _attn_probs(s_refs[cur], p_refs[cur], l_refs[cur], c)
    last = (nq - 1) % 2
    o_ref[rows(nq - 1), :] = _attn_output(p_refs[last], l_refs[last], kv_refs, n_kv).astype(o_ref.dtype)


def _attn_scratch(tq, n_keys):
    return ([pltpu.VMEM((tq, n_keys), F32)] * 2 + [pltpu.VMEM((tq, n_keys), BF16)] * 2
            + [pltpu.VMEM((tq, LANES), F32)] * 2)


def _mla_attention(lay, q, k, v, heads, scale, tq, tk):
    b, s, c = lay.batch, lay.seq, lay.ctx
    qk_w, v_w = 2 * LANES, LANES
    ctx_blk0 = lay.n_lat // c
    lat_chunks = tuple([(0, 0, c)] + [(1, st, tk) for st in range(0, s, tk)])
    assert s % tq == 0 and s % tk == 0
    o_lat = pl.pallas_call(
        functools.partial(_attn_kernel, chunks=lat_chunks, n_kv=2, scale=scale, tq=tq),
        scratch_shapes=_attn_scratch(tq, c + s),
        grid=(b, heads),
        in_specs=[pl.BlockSpec((s, qk_w), lambda bi, h: (bi, h)),
                  pl.BlockSpec((c, qk_w), lambda bi, h: (ctx_blk0 + bi, h)),
                  pl.BlockSpec((c, v_w), lambda bi, h: (ctx_blk0 + bi, h)),
                  pl.BlockSpec((s, qk_w), lambda bi, h: (bi, h)),
                  pl.BlockSpec((s, v_w), lambda bi, h: (bi, h))],
        out_specs=pl.BlockSpec((s, v_w), lambda bi, h: (bi, h)),
        out_shape=jax.ShapeDtypeStruct((lay.n_lat, heads * v_w), BF16),
        compiler_params=_params(2),
        name="mla_attn_latent",
    )(q, k, v, k, v)
    o_ctx = pl.pallas_call(
        functools.partial(_attn_kernel, chunks=((0, 0, c),), n_kv=1, scale=scale, tq=c),
        scratch_shapes=_attn_scratch(c, c),
        grid=(b, heads),
        in_specs=[pl.BlockSpec((c, qk_w), lambda bi, h: (ctx_blk0 + bi, h)),
                  pl.BlockSpec((c, qk_w), lambda bi, h: (ctx_blk0 + bi, h)),
                  pl.BlockSpec((c, v_w), lambda bi, h: (ctx_blk0 + bi, h))],
        out_specs=pl.BlockSpec((c, v_w), lambda bi, h: (bi, h)),
        out_shape=jax.ShapeDtypeStruct((b * c, heads * v_w), BF16),
        compiler_params=_params(2),
        name="mla_attn_ctx",
    )(q, k, v)
    return o_lat, o_ctx


def _mla_mixer(lay, h, tok, mods, layer, slot, wq_a_all, q_norm, wq_b, wkv_a, kv_norm, wkv_b, wo_all):
    d, tm = lay.d, lay.tm
    q_lora = wq_a_all.shape[2]
    wq_a = (wq_a_all, slot)
    heads, qk = wq_b.shape[1], wq_b.shape[2]
    kv_lora = kv_norm.shape[0]
    rope = wkv_a.shape[1] - kv_lora
    nope = qk - rope
    v_dim = wkv_b.shape[2] - nope
    assert nope == LANES and v_dim == LANES and rope <= LANES and rope % 4 == 0
    cos_t, sin_t = _mla_rope_tables(lay, rope)

    cq = _linear(h, wq_a, tm=tm, tn=min(512, q_lora), dtype=F32, name="mla_q_a")
    wkv_a_pad = jnp.pad(wkv_a, ((0, 0), (0, LANES - rope)))
    kv = _linear(h, wkv_a_pad, tm=tm, tn=kv_lora + LANES, dtype=F32, name="mla_kv_a")
    cqn, ckv, kr = _mla_norm(lay, cq, kv, q_norm, kv_norm, cos_t, sin_t, kv_lora, rope)

    hpt = 2
    wq = jnp.pad(wq_b, ((0, 0), (0, 0), (0, 2 * LANES - qk))).reshape(q_lora, heads * 2 * LANES)
    tn_q = hpt * 2 * LANES
    q = _fused_matmul(
        cqn, [_w2d(wq, tn_q)],
        [(cos_t, pl.BlockSpec((tm, LANES), lambda j, i: (i, 0))), (sin_t, pl.BlockSpec((tm, LANES), lambda j, i: (i, 0)))],
        [(jax.ShapeDtypeStruct((lay.n_tok, heads * 2 * LANES), BF16), _tile_spec(tm, tn_q))],
        functools.partial(_ep_mla_q, heads_per_tile=hpt, rope=rope),
        tm=tm, tn=tn_q, n_col_blocks=heads // hpt, name="mla_q_b")[0]

    wk = wkv_b[:, :, :nope].reshape(kv_lora, heads * nope)
    wv = wkv_b[:, :, nope:].reshape(kv_lora, heads * v_dim)
    hpt_k = 4
    k = _fused_matmul(
        ckv, [_w2d(wk, hpt_k * LANES)],
        [(kr, pl.BlockSpec((tm, LANES), lambda j, i: (i, 0)))],
        [(jax.ShapeDtypeStruct((lay.n_tok, heads * 2 * LANES), BF16), _tile_spec(tm, hpt_k * 2 * LANES))],
        functools.partial(_ep_mla_k, heads_per_tile=hpt_k),
        tm=tm, tn=hpt_k * LANES, n_col_blocks=heads // hpt_k, name="mla_k_b")[0]
    v = _linear(ckv, wv, tm=tm, tn=min(512, heads * v_dim), dtype=BF16, name="mla_v_b")

    o = _mla_attention(lay, q, k, v, heads, qk ** -0.5, tq=min(512, lay.seq // 2), tk=min(512, lay.seq))
    wo = (wo_all.reshape(wo_all.shape[0], heads * v_dim, d), slot)
    return _residual_linear(lay, o, wo, tok, mods, layer, 2, tn=min(512, d), name="mla_o")


def _ep_rotate(accs, e_refs, o_refs, *, heads_per_tile, dk, scale):
    cos_ref, sin_ref = e_refs
    cos, sin = cos_ref[...], sin_ref[...]
    acc = accs[0]
    half = dk // 2
    for h in range(heads_per_tile):
        x1 = acc[:, h * dk:h * dk + half]
        x2 = acc[:, h * dk + half:(h + 1) * dk]
        o_refs[0][:, h * dk:h * dk + half] = ((x1 * cos - x2 * sin) * scale).astype(BF16)
        o_refs[0][:, h * dk + half:(h + 1) * dk] = ((x1 * sin + x2 * cos) * scale).astype(BF16)


def _rotated_linear(lay, a, w, cos, sin, dk, scale, name):
    tm = lay.tm
    n = w.shape[1]
    tn = 2 * dk
    half = dk // 2
    return _fused_matmul(
        a, [_w2d(w, tn)],
        [(cos, pl.BlockSpec((tm, half), lambda j, i: (i, 0))), (sin, pl.BlockSpec((tm, half), lambda j, i: (i, 0)))],
        [(jax.ShapeDtypeStruct((lay.n_tok, n), BF16), _tile_spec(tm, tn))],
        functools.partial(_ep_rotate, heads_per_tile=tn // dk, dk=dk, scale=scale),
        tm=tm, tn=tn, n_col_blocks=n // tn, name=name)[0]


def _retention_kernel(lg_ref, ql_ref, kl_ref, vl_ref, qc_ref, kc_ref, vc_ref, yl_ref, yc_ref,
                      sf_ref, sb_ref, yfl_ref, yfc_ref, ybl_ref, ybc_ref, *, chunk, heads):
    h = pl.program_id(1)
    lg_f = lg_ref[h]
    lg_b = lg_ref[heads + h]
    n_lat = ql_ref.shape[0] // chunk
    n_ctx = qc_ref.shape[0] // chunk
    dv = vl_ref.shape[1]

    row = lax.broadcasted_iota(jnp.int32, (chunk, chunk), 0).astype(F32)
    col = lax.broadcasted_iota(jnp.int32, (chunk, chunk), 1).astype(F32)
    pos = lax.broadcasted_iota(jnp.int32, (chunk, dv), 0).astype(F32)
    diff = row - col
    dmat_f = jnp.where(diff >= 0, jnp.exp(jnp.where(diff >= 0, diff, 0.0) * lg_f), 0.0)
    dmat_b = jnp.where(diff <= 0, jnp.exp(jnp.where(diff <= 0, -diff, 0.0) * lg_b), 0.0)
    qdec_f = jnp.exp((pos + 1.0) * lg_f)
    kdec_f = jnp.exp((chunk - 1.0 - pos) * lg_f)
    qdec_b = jnp.exp((chunk - pos) * lg_b)
    kdec_b = jnp.exp(pos * lg_b)
    blk_f = jnp.exp(chunk * lg_f)
    blk_b = jnp.exp(chunk * lg_b)

    def step(q, k, v, s_ref, dmat, qdec, kdec, blk):
        scores = lax.dot_general(q, k, (((1,), (1,)), ((), ())), preferred_element_type=F32) * dmat
        inner = jnp.dot(scores.astype(BF16), v, preferred_element_type=F32)
        state = s_ref[...]
        cross = jnp.dot(q, state.astype(BF16), preferred_element_type=F32) * qdec
        kd = (k.astype(F32) * kdec).astype(BF16)
        s_ref[...] = state * blk + lax.dot_general(kd, v, (((0,), (0,)), ((), ())), preferred_element_type=F32)
        return inner + cross

    def fwd(q_ref, k_ref, v_ref, y_ref, t):
        sl = pl.ds(pl.multiple_of(t * chunk, chunk), chunk)
        y_ref[sl, :] = step(q_ref[sl, :], k_ref[sl, :], v_ref[sl, :], sf_ref, dmat_f, qdec_f, kdec_f, blk_f)

    def bwd(q_ref, k_ref, v_ref, y_ref, t):
        sl = pl.ds(pl.multiple_of(t * chunk, chunk), chunk)
        y_ref[sl, :] = step(q_ref[sl, :], k_ref[sl, :], v_ref[sl, :], sb_ref, dmat_b, qdec_b, kdec_b, blk_b)

    sf_ref[...] = jnp.zeros_like(sf_ref)
    sb_ref[...] = jnp.zeros_like(sb_ref)
    for t in range(n_ctx):
        fwd(qc_ref, kc_ref, vc_ref, yfc_ref, t)
        bwd(qc_ref, kc_ref, vc_ref, ybc_ref, n_ctx - 1 - t)

    def body(t, carry):
        fwd(ql_ref, kl_ref, vl_ref, yfl_ref, t)
        bwd(ql_ref, kl_ref, vl_ref, ybl_ref, n_lat - 1 - t)
        return carry

    lax.fori_loop(0, n_lat, body, 0)

    def group_norm(y):
        mu = jnp.mean(y, axis=-1, keepdims=True)
        var = jnp.mean(jnp.square(y - mu), axis=-1, keepdims=True)
        return (y - mu) * lax.rsqrt(var + GROUP_NORM_EPS)

    for t in range(n_ctx):
        sl = pl.ds(t * chunk, chunk)
        yc_ref[sl, :] = group_norm(yfc_ref[sl, :] + ybc_ref[sl, :]).astype(BF16)

    def norm_body(t, carry):
        sl = pl.ds(pl.multiple_of(t * chunk, chunk), chunk)
        yl_ref[sl, :] = group_norm(yfl_ref[sl, :] + ybl_ref[sl, :]).astype(BF16)
        return carry

    lax.fori_loop(0, n_lat, norm_body, 0)


def _retention(lay, q, k, v, log_g, heads, dk, dv, chunk):
    b, s, c = lay.batch, lay.seq, lay.ctx
    ctx_blk0 = lay.n_lat // c
    lat = lambda w: pl.BlockSpec((s, w), lambda bi, h, lg: (bi, h))
    ctx = lambda w: pl.BlockSpec((c, w), lambda bi, h, lg: (ctx_blk0 + bi, h))
    y_lat, y_ctx = pl.pallas_call(
        functools.partial(_retention_kernel, chunk=chunk, heads=heads),
        grid_spec=pltpu.PrefetchScalarGridSpec(
            num_scalar_prefetch=1,
            grid=(b, heads),
            in_specs=[lat(dk), lat(dk), lat(dv), ctx(dk), ctx(dk), ctx(dv)],
            out_specs=[lat(dv), pl.BlockSpec((c, dv), lambda bi, h, lg: (bi, h))],
            scratch_shapes=[pltpu.VMEM((dk, dv), F32), pltpu.VMEM((dk, dv), F32),
                            pltpu.VMEM((s, dv), F32), pltpu.VMEM((c, dv), F32),
                            pltpu.VMEM((s, dv), F32), pltpu.VMEM((c, dv), F32)]),
        out_shape=[jax.ShapeDtypeStruct((lay.n_lat, heads * dv), BF16),
                   jax.ShapeDtypeStruct((b * c, heads * dv), BF16)],
        compiler_params=_params(2),
        name="retention",
    )(log_g.reshape(-1), q, k, v, q, k, v)
    return y_lat, y_ctx


def _ep_gate_mul(accs, e_refs, o_refs, *, lat_tiles):
    y_lat_ref, y_ctx_ref = e_refs
    y = jnp.where(pl.program_id(1) < lat_tiles, y_lat_ref[...], y_ctx_ref[...])
    o_refs[0][...] = (_silu(accs[0]) * y.astype(F32)).astype(BF16)


def _retention_mixer(lay, h, tok, mods, layer, wq, wk, wv, wg, wo, decay):
    d, tm = lay.d, lay.tm
    heads, dk = wq.shape[1], wq.shape[2]
    dv = wv.shape[2]
    assert dk == dv and dk % (2 * LANES) == 0
    chunk = min(256, lay.ctx)
    assert lay.ctx % chunk == 0 and lay.seq % chunk == 0
    log_g = -jnp.exp(decay.astype(F32))
    cos, sin = _ret_tables(lay, dk)
    q = _rotated_linear(lay, h, wq.reshape(d, heads * dk), cos, sin, dk, 1.0, "ret_q")
    k = _rotated_linear(lay, h, wk.reshape(d, heads * dk), cos, sin, dk, dk ** -0.5, "ret_k")
    v = _linear(h, wv.reshape(d, heads * dv), tm=tm, tn=min(512, heads * dv), dtype=BF16, name="ret_v")
    y_lat, y_ctx = _retention(lay, q, k, v, log_g, heads, dk, dv, chunk)
    tn = min(512, heads * dv)
    lat_tiles = lay.lat_tiles
    z = _fused_matmul(h, [_w2d(wg, tn)],
                      [(y_lat, pl.BlockSpec((tm, tn), lambda j, i: (jnp.minimum(i, lat_tiles - 1), j))),
                       (y_ctx, pl.BlockSpec((tm, tn), lambda j, i: (0, j)))],
                      [(jax.ShapeDtypeStruct((lay.n_tok, heads * dv), BF16), _tile_spec(tm, tn))],
                      functools.partial(_ep_gate_mul, lat_tiles=lat_tiles),
                      tm=tm, tn=tn, n_col_blocks=heads * dv // tn, name="ret_gate")[0]
    return _residual_linear(lay, z, wo, tok, mods, layer, 2, tn=min(512, d), name="ret_o")


def _ep_glu(accs, e_refs, o_refs):
    ba_ref, bg_ref = e_refs
    o_refs[0][...] = (accs[0] + ba_ref[...]) * jax.nn.sigmoid(accs[1] + bg_ref[...])


HALO = 16


SUBLANES = 8


def _dwconv_kernel(prev_ref, cur_ref, next_ref, w_ref, wb_ref, g_ref, b_ref, o_ref, buf_ref, acc_ref, *,
                   width, tt, tiles_per_seq, tiles_per_ctx, lat_tiles):
    i = pl.program_id(0)
    in_lat = i < lat_tiles
    pos = jnp.where(in_lat, i % tiles_per_seq, (i - lat_tiles) % tiles_per_ctx)
    n_seq_tiles = jnp.where(in_lat, tiles_per_seq, tiles_per_ctx)
    first = pos == 0
    last = pos == n_seq_tiles - 1
    buf_ref[0:HALO, :] = jnp.where(first, 0.0, prev_ref[...])
    buf_ref[HALO:HALO + tt, :] = cur_ref[...]
    buf_ref[HALO + tt:HALO + tt + HALO, :] = jnp.where(last, 0.0, next_ref[...])
    lead = HALO - width // 2
    n_out = tt // SUBLANES
    max_dblk = (lead + width - 1) // SUBLANES
    assert (n_out + max_dblk + 1) * SUBLANES <= tt + 2 * HALO
    sub = lax.broadcasted_iota(jnp.int32, (SUBLANES, LANES), 0)

    def lane_chunk(cidx, carry):
        ls = pl.ds(pl.multiple_of(cidx * LANES, LANES), LANES)
        bias = jnp.broadcast_to(wb_ref[:, ls], (SUBLANES, LANES))
        taps = [jnp.broadcast_to(w_ref[k:k + 1, ls], (SUBLANES, LANES)) for k in range(width)]

        def load(blk):
            v = buf_ref[blk * SUBLANES:(blk + 1) * SUBLANES, ls]
            return v, {s: pltpu.roll(v, SUBLANES - s, 0) for s in range(1, SUBLANES)}

        accs = {}
        nxt = load(0)
        for blk in range(n_out + max_dblk):
            (va, rolls_a), nxt = nxt, load(blk + 1)
            for s in range(SUBLANES):
                x = va if s == 0 else jnp.where(sub < SUBLANES - s, rolls_a[s], nxt[1][s])
                for dblk in range(max_dblk + 1):
                    k = SUBLANES * dblk + s - lead
                    out_blk = blk - dblk
                    if 0 <= k < width and 0 <= out_blk < n_out:
                        accs[out_blk] = accs.get(out_blk, bias) + x * taps[k]
            done = blk - max_dblk
            if done >= 0:
                acc_ref[done * SUBLANES:(done + 1) * SUBLANES, ls] = accs.pop(done)
        assert not accs
        return carry

    lax.fori_loop(0, cur_ref.shape[1] // LANES, lane_chunk, 0)

    ln_rows = 2 * SUBLANES

    def ln_block(rb, carry):
        rs = pl.ds(pl.multiple_of(rb * ln_rows, ln_rows), ln_rows)
        acc = acc_ref[rs, :]
        mu = jnp.mean(acc, axis=-1, keepdims=True)
        cen = acc - mu
        var = jnp.mean(cen * cen, axis=-1, keepdims=True)
        y = cen * lax.rsqrt(var + NORM_EPS) * g_ref[...] + b_ref[...]
        o_ref[rs, :] = _silu(y).astype(BF16)
        return carry

    lax.fori_loop(0, tt // ln_rows, ln_block, 0, unroll=4)


def _dwconv_ln_swish(lay, u, dw, dw_b, ln_g, ln_b):
    d = lay.d
    width = dw.shape[0]
    assert width // 2 <= HALO
    tt = min(256, lay.ctx)
    assert lay.ctx % tt == 0 and lay.seq % tt == 0 and tt % HALO == 0
    r = tt // HALO
    n_tiles = lay.n_tok // tt
    n_halo_blocks = lay.n_tok // HALO
    const = lambda rows: pl.BlockSpec((rows, d), lambda i: (0, 0))
    kern = functools.partial(_dwconv_kernel, width=width, tt=tt, tiles_per_seq=lay.seq // tt,
                             tiles_per_ctx=lay.ctx // tt, lat_tiles=lay.n_lat // tt)
    return pl.pallas_call(
        kern,
        grid=(n_tiles,),
        in_specs=[pl.BlockSpec((HALO, d), lambda i: (jnp.maximum(i * r - 1, 0), 0)),
                  pl.BlockSpec((tt, d), lambda i: (i, 0)),
                  pl.BlockSpec((HALO, d), lambda i: (jnp.minimum((i + 1) * r, n_halo_blocks - 1), 0)),
                  const(width), const(1), const(1), const(1)],
        out_specs=pl.BlockSpec((tt, d), lambda i: (i, 0)),
        out_shape=jax.ShapeDtypeStruct((lay.n_tok, d), BF16),
        scratch_shapes=[pltpu.VMEM((tt + 2 * HALO, d), F32), pltpu.VMEM((tt, d), F32)],
        compiler_params=_params(1),
        name="dwconv_ln_swish",
    )(u, u, u, dw, dw_b.reshape(1, d), ln_g.reshape(1, d), ln_b.reshape(1, d))


def _conformer_mixer(lay, h, tok, mods, layer, pw1, b1, dw, dw_b, ln_g, ln_b, pw2, b2):
    d, tm = lay.d, lay.tm
    tn = min(256, d)
    cols = d // tn
    b1r = b1.reshape(1, 2 * d)
    u = _fused_matmul(
        h,
        [(pw1, pl.BlockSpec((d, tn), lambda j, i: (0, j))), (pw1, pl.BlockSpec((d, tn), lambda j, i: (0, cols + j)))],
        [(b1r, pl.BlockSpec((1, tn), lambda j, i: (0, j))), (b1r, pl.BlockSpec((1, tn), lambda j, i: (0, cols + j)))],
        [(jax.ShapeDtypeStruct((lay.n_tok, d), F32), _tile_spec(tm, tn))],
        _ep_glu, tm=tm, tn=tn, n_col_blocks=cols, name="conv_pw1_glu")[0]
    a = _dwconv_ln_swish(lay, u, dw, dw_b, ln_g, ln_b)
    return _residual_linear(lay, a, pw2, tok, mods, layer, 2, tn=min(512, d), bias=b2, name="conv_pw2")


MOE_ROW_TILE = 256


def _dispatch_plan(idx, wts, n_exp, tmx):
    eid = idx[:, :MOE_TOP_K].reshape(-1)
    w = wts[:, :MOE_TOP_K].reshape(-1)
    n_pairs = eid.shape[0]
    onehot = (eid[:, None] == jnp.arange(n_exp, dtype=jnp.int32)[None, :]).astype(jnp.int32)
    csum = jnp.cumsum(onehot, axis=0)
    rank = jnp.sum((csum - onehot) * onehot, axis=1)
    counts = csum[-1]
    padded = ((counts + tmx - 1) // tmx) * tmx
    ends = jnp.cumsum(padded)
    starts = ends - padded
    pos = (starts[eid] + rank).astype(jnp.int32)
    n_tiles = (n_pairs + n_exp * (tmx - 1)) // tmx + 1
    n_slots = n_tiles * tmx
    n_used = (ends[-1] // tmx).astype(jnp.int32).reshape(1)
    tile_start = jnp.arange(n_tiles, dtype=jnp.int32) * tmx
    tile_expert = jnp.sum((tile_start[:, None] >= ends[None, :]).astype(jnp.int32), axis=1)
    tile_expert = jnp.minimum(tile_expert, n_exp - 1).astype(jnp.int32)
    slot_pair1 = jnp.zeros((n_slots,), jnp.int32).at[pos].set(jnp.arange(1, n_pairs + 1, dtype=jnp.int32),
                                                              unique_indices=True)
    slot_pair = jnp.maximum(slot_pair1 - 1, 0)
    src_token = slot_pair // MOE_TOP_K
    slot_w = jnp.where(slot_pair1 > 0, w[slot_pair], 0.0)
    pos_by_choice = pos.reshape(-1, MOE_TOP_K).T.reshape(-1)
    return pos_by_choice, src_token, slot_w.reshape(n_slots, 1), tile_expert, n_used


def _row_gather(src_hbm, dst, sem, index_of_row, n_rows, unroll=8):
    def body(r, carry):
        pltpu.make_async_copy(src_hbm.at[pl.ds(index_of_row(r), 1), :], dst.at[pl.ds(r, 1), :], sem).start()
        return carry
    lax.fori_loop(0, n_rows, body, 0, unroll=unroll)


def _row_gather_wait(src_hbm, dst, sem):
    pltpu.make_async_copy(src_hbm.at[pl.ds(0, dst.shape[0]), :], dst, sem).wait()


def _moe_up_kernel(te_ref, src_ref, nu_ref, hp_hbm, w1_ref, w3_ref, sw_ref, act_ref, buf, sem, w1b, w3b, *, tmx):
    t = pl.program_id(0)
    n_used = nu_ref[0]

    def start_tile(tile, slot, unroll):
        _row_gather(hp_hbm, buf.at[slot], sem.at[slot], lambda r: src_ref[tile * tmx + r], tmx, unroll)

    @pl.when(t == 0)
    def _():
        start_tile(0, 0, 8)

    @pl.when(t < n_used)
    def _():
        slot = t % 2
        new_expert = jnp.logical_or(t == 0, te_ref[t] != te_ref[jnp.maximum(t - 1, 0)])

        @pl.when(new_expert)
        def _():
            w1b[...] = w1_ref[...].astype(BF16)
            w3b[...] = w3_ref[...].astype(BF16)

        _row_gather_wait(hp_hbm, buf.at[slot], sem.at[slot])
        start_tile(jnp.minimum(t + 1, n_used - 1), 1 - slot, True)
        a_lo, a_hi = _unpack_bf16_pairs(buf[slot])
        half = a_lo.shape[1]
        u1 = (jnp.dot(a_lo, w1b[:half, :], preferred_element_type=F32)
              + jnp.dot(a_hi, w1b[half:, :], preferred_element_type=F32))
        u3 = (jnp.dot(a_lo, w3b[:half, :], preferred_element_type=F32)
              + jnp.dot(a_hi, w3b[half:, :], preferred_element_type=F32))
        act_ref[...] = (_silu(u1) * u3 * sw_ref[...]).astype(BF16)

    @pl.when(t == n_used)
    def _():
        _row_gather_wait(hp_hbm, buf.at[t % 2], sem.at[t % 2])

    @pl.when(t >= n_used)
    def _():
        act_ref[...] = jnp.zeros_like(act_ref)


def _moe_down_kernel(te_ref, nu_ref, act_ref, w2_ref, y_ref, w2b):
    t = pl.program_id(0)
    n_used = nu_ref[0]

    @pl.when(t < n_used)
    def _():
        new_expert = jnp.logical_or(t == 0, te_ref[t] != te_ref[jnp.maximum(t - 1, 0)])

        @pl.when(new_expert)
        def _():
            w2b[...] = w2_ref[...].astype(BF16)

        y_ref[...] = jnp.dot(act_ref[...], w2b[...], preferred_element_type=F32)

    @pl.when(t >= n_used)
    def _():
        y_ref[...] = jnp.zeros_like(y_ref)


def _moe_combine_kernel(pos_ref, y_hbm, tok_ref, gate_ref, o_ref, buf, sem, *, tmc, n_steps):
    i = pl.program_id(0)

    def start_tile(tile, slot):
        for choice in range(MOE_TOP_K):
            base = choice * n_steps * tmc + tile * tmc
            _row_gather(y_hbm, buf.at[slot, pl.ds(choice * tmc, tmc)], sem.at[slot],
                        lambda r, base=base: pos_ref[base + r], tmc)

    @pl.when(i == 0)
    def _():
        start_tile(0, 0)

    @pl.when(i + 1 < n_steps)
    def _():
        start_tile(i + 1, (i + 1) % 2)

    slot = i % 2
    _row_gather_wait(y_hbm, buf.at[slot], sem.at[slot])
    o_ref[...] = tok_ref[...] + gate_ref[...] * (buf[slot, 0:tmc, :] + buf[slot, tmc:2 * tmc, :])


def _moe_routed(lay, hp, idx, wts, tok, mods, layer, w1, w3, w2):
    d = lay.d
    n_exp, d_ff = w1.shape[1], w1.shape[3]
    tmx = MOE_ROW_TILE
    tmc = lay.tm // 2
    assert MOE_TOP_K == 2 and lay.n_tok % tmc == 0
    pos, src_token, slot_w, tile_expert, n_used = _dispatch_plan(idx, wts, n_exp, tmx)
    n_slots = src_token.shape[0]
    n_tiles = n_slots // tmx

    act = pl.pallas_call(
        functools.partial(_moe_up_kernel, tmx=tmx),
        grid_spec=pltpu.PrefetchScalarGridSpec(
            num_scalar_prefetch=3,
            grid=(n_tiles,),
            in_specs=[pl.BlockSpec(memory_space=pl.ANY),
                      pl.BlockSpec((None, None, d, d_ff), lambda t, te, src, nu: (layer, te[t], 0, 0)),
                      pl.BlockSpec((None, None, d, d_ff), lambda t, te, src, nu: (layer, te[t], 0, 0)),
                      pl.BlockSpec((tmx, 1), lambda t, te, src, nu: (t, 0))],
            out_specs=pl.BlockSpec((tmx, d_ff), lambda t, te, src, nu: (t, 0)),
            scratch_shapes=[pltpu.VMEM((2, tmx, d // 2), jnp.uint32), pltpu.SemaphoreType.DMA((2,)),
                            pltpu.VMEM((d, d_ff), BF16), pltpu.VMEM((d, d_ff), BF16)]),
        out_shape=jax.ShapeDtypeStruct((n_slots, d_ff), BF16),
        compiler_params=_params(1),
        name="moe_up",
    )(tile_expert, src_token, n_used, hp, w1, w3, slot_w)

    y = pl.pallas_call(
        _moe_down_kernel,
        grid_spec=pltpu.PrefetchScalarGridSpec(
            num_scalar_prefetch=2,
            grid=(n_tiles,),
            in_specs=[pl.BlockSpec((tmx, d_ff), lambda t, te, nu: (t, 0)),
                      pl.BlockSpec((None, None, d_ff, d), lambda t, te, nu: (layer, te[t], 0, 0))],
            out_specs=pl.BlockSpec((tmx, d), lambda t, te, nu: (t, 0)),
            scratch_shapes=[pltpu.VMEM((d_ff, d), BF16)]),
        out_shape=jax.ShapeDtypeStruct((n_slots, d), F32),
        compiler_params=_params(1),
        name="moe_down",
    )(tile_expert, n_used, act, w2)

    n_steps = lay.n_tok // tmc
    gate_rows = lay.tm // tmc
    return pl.pallas_call(
        functools.partial(_moe_combine_kernel, tmc=tmc, n_steps=n_steps),
        grid_spec=pltpu.PrefetchScalarGridSpec(
            num_scalar_prefetch=1,
            grid=(n_steps,),
            in_specs=[pl.BlockSpec(memory_space=pl.ANY),
                      pl.BlockSpec((tmc, d), lambda i, pos: (i, 0)),
                      pl.BlockSpec((None, 1, d), lambda i, pos: (_mod_row(lay, layer, i // gate_rows), 0, 5))],
            out_specs=pl.BlockSpec((tmc, d), lambda i, pos: (i, 0)),
            scratch_shapes=[pltpu.VMEM((2, MOE_TOP_K * tmc, d), F32), pltpu.SemaphoreType.DMA((2,))]),
        out_shape=jax.ShapeDtypeStruct((lay.n_tok, d), F32),
        compiler_params=_params(1),
        name="moe_combine",
    )(pos, y, tok, mods)


def kernel(x, c, ctx, c_ctx, ada_w, ada_b, norm_mix, norm_ffn, mla_wq_a, mla_q_norm, mla_wq_b, mla_wkv_a, mla_kv_norm, mla_wkv_b, mla_wo, ret_wq, ret_wk, ret_wv, ret_wg, ret_wo, ret_decay, conv_pw1, conv_b1, conv_dw, conv_dw_b, conv_ln_g, conv_ln_b, conv_pw2, conv_b2, moe_wg_router, moe_bg_router, moe_we_router, moe_be_router, moe_w1, moe_w3, moe_w2, final_norm):
    b, s, d = x.shape
    n_ctx = ctx.shape[1]
    depth = ada_w.shape[0]
    n_mixers = 3
    lay = Layout(batch=b, seq=s, ctx=n_ctx, d=d, tm=b * n_ctx)
    assert s % lay.tm == 0 and b + 1 <= MOD_ROWS and s % GRID_W == 0

    tok = jnp.concatenate([x.reshape(b * s, d), ctx.reshape(b * n_ctx, d)], axis=0)
    cond = jnp.zeros((MOD_ROWS, d), F32).at[:b].set(c).at[b].set(c_ctx)
    mods = _adaln(cond, ada_w, ada_b, tk=min(LANES, d)).reshape(depth * MOD_ROWS, 1, N_MOD * d)

    for i in range(depth):
        kind, slot = i % n_mixers, i // n_mixers
        h = _normmod(lay, tok, norm_mix[i], mods, i, 0, 1)
        if kind == 0:
            tok = _mla_mixer(lay, h, tok, mods, i, slot, mla_wq_a, mla_q_norm[slot], mla_wq_b[slot], mla_wkv_a[slot],
                             mla_kv_norm[slot], mla_wkv_b[slot], mla_wo)
        elif kind == 1:
            tok = _retention_mixer(lay, h, tok, mods, i, ret_wq[slot], ret_wk[slot], ret_wv[slot], ret_wg[slot],
                                   ret_wo[slot], ret_decay[slot])
        else:
            tok = _conformer_mixer(lay, h, tok, mods, i, conv_pw1[slot], conv_b1[slot], conv_dw[slot], conv_dw_b[slot],
                                   conv_ln_g[slot], conv_ln_b[slot], conv_pw2[slot], conv_b2[slot])
        hp, idx, wts = _normmod_router(lay, tok, norm_ffn[i], mods, i, 3, 4, moe_wg_router[i], moe_bg_router[i],
                                       moe_we_router[i], moe_be_router[i])
        tok = _moe_routed(lay, hp, idx, wts, tok, mods, i, moe_w1, moe_w3, moe_w2)
    return _final_norm(lay, tok, final_norm).reshape(b, s, d)
```

```python
import functools
import math
from typing import NamedTuple

import jax
import jax.numpy as jnp
from jax import lax
from jax.experimental import pallas as pl
from jax.experimental.pallas import tpu as pltpu

F32 = jnp.float32
BF16 = jnp.bfloat16

GRID_W = 64
ROPE_BASE = 10000.0
RET_THETA_BASE = 10000.0
NORM_EPS = 1e-6
GROUP_NORM_EPS = 1e-5
N_MOD = 6
MOE_TOP_K = 2

LANES = 128
MOD_ROWS = 8
VMEM_LIMIT_BYTES = 56 * 1024 * 1024
LOG2E = math.log2(math.e)


class Layout(NamedTuple):
    batch: int
    seq: int
    ctx: int
    d: int
    tm: int

    @property
    def n_lat(self):
        return self.batch * self.seq

    @property
    def n_tok(self):
        return self.batch * (self.seq + self.ctx)

    @property
    def lat_tiles(self):
        return self.n_lat // self.tm

    @property
    def row_tiles(self):
        return self.n_tok // self.tm

    @property
    def tiles_per_batch(self):
        return self.seq // self.tm


def _params(n_axes):
    return pltpu.CompilerParams(dimension_semantics=("arbitrary",) * n_axes,
                                vmem_limit_bytes=VMEM_LIMIT_BYTES)


def _mod_row(lay, layer, i):
    return layer * MOD_ROWS + jnp.minimum(i // lay.tiles_per_batch, lay.batch)


def _silu(v):
    return v * jax.nn.sigmoid(v)


def _adaln_kernel(c_ref, w_ref, b_ref, o_ref):
    @pl.when(pl.program_id(1) == 0)
    def _():
        o_ref[...] = jnp.broadcast_to(b_ref[...], o_ref.shape)

    a = _silu(c_ref[...]).astype(BF16)
    o_ref[...] += jnp.dot(a, w_ref[...].astype(BF16), preferred_element_type=F32)


def _adaln(cond, ada_w, ada_b, tk):
    n_layers, d, n = ada_w.shape
    return pl.pallas_call(
        _adaln_kernel,
        grid=(n_layers, d // tk),
        in_specs=[pl.BlockSpec((MOD_ROWS, tk), lambda l, k: (0, k)),
                  pl.BlockSpec((None, tk, n), lambda l, k: (l, k, 0)),
                  pl.BlockSpec((None, 1, n), lambda l, k: (l, 0, 0))],
        out_specs=pl.BlockSpec((None, MOD_ROWS, n), lambda l, k: (l, 0, 0)),
        out_shape=jax.ShapeDtypeStruct((n_layers, MOD_ROWS, n), F32),
        compiler_params=_params(2),
        name="adaln",
    )(cond, ada_w, ada_b.reshape(n_layers, 1, n))


def _rms(x, g):
    return x * lax.rsqrt(jnp.mean(x * x, axis=-1, keepdims=True) + NORM_EPS) * g


def _normmod_kernel(t_ref, g_ref, sh_ref, sc_ref, h_ref):
    h = _rms(t_ref[...], g_ref[...]) * (1.0 + sc_ref[...]) + sh_ref[...]
    h_ref[...] = h.astype(BF16)


def _route(logits, n_groups, per_group):
    n_exp = n_groups * per_group
    lane = lax.broadcasted_iota(jnp.int32, logits.shape, 1)
    neg = jnp.float32(-jnp.inf)
    is_g = (lane >= n_exp) & (lane < n_exp + n_groups)
    lg = jnp.where(is_g, logits, neg)
    mg = jnp.max(lg, axis=1, keepdims=True)
    g_sel = jnp.min(jnp.where(lg == mg, lane, LANES), axis=1, keepdims=True) - n_exp
    pg_sel = 1.0 / jnp.sum(jnp.where(is_g, jnp.exp(lg - mg), 0.0), axis=1, keepdims=True)
    in_sel = (lane >= g_sel * per_group) & (lane < (g_sel + 1) * per_group)
    le = jnp.where(in_sel, logits, neg)
    m1 = jnp.max(le, axis=1, keepdims=True)
    i1 = jnp.min(jnp.where(le == m1, lane, LANES), axis=1, keepdims=True)
    le2 = jnp.where(lane == i1, neg, le)
    m2 = jnp.max(le2, axis=1, keepdims=True)
    i2 = jnp.min(jnp.where(le2 == m2, lane, LANES), axis=1, keepdims=True)
    e2 = jnp.exp(m2 - m1)
    w1 = pg_sel / (1.0 + e2)
    w2 = pg_sel * e2 / (1.0 + e2)
    idx = jnp.where(lane == 0, i1, jnp.where(lane == 1, i2, 0))
    wts = jnp.where(lane == 0, w1, jnp.where(lane == 1, w2, 0.0))
    return idx, wts


def _pack_bf16_pairs(h):
    half = h.shape[1] // 2
    hb = h.astype(BF16).astype(F32)
    lo = lax.shift_right_logical(lax.bitcast_convert_type(hb[:, :half], jnp.uint32), jnp.uint32(16))
    hi = lax.bitcast_convert_type(hb[:, half:], jnp.uint32) & jnp.uint32(0xFFFF0000)
    return hi | lo


def _unpack_pairs_f32(w):
    lo = lax.bitcast_convert_type(lax.shift_left(w, jnp.uint32(16)), F32)
    hi = lax.bitcast_convert_type(w & jnp.uint32(0xFFFF0000), F32)
    return lo, hi


def _unpack_bf16_pairs(w):
    lo, hi = _unpack_pairs_f32(w)
    return lo.astype(BF16), hi.astype(BF16)


def _normmod_router_kernel(t_ref, g_ref, sh_ref, sc_ref, w2_ref, wh_ref, rb_ref, hp_ref, idx_ref, wts_ref, *,
                           n_groups, per_group):
    h = _rms(t_ref[...], g_ref[...]) * (1.0 + sc_ref[...]) + sh_ref[...]
    hp_ref[...] = _pack_bf16_pairs(h)
    h_hi = h.astype(BF16)
    h_lo = (h - h_hi.astype(F32)).astype(BF16)
    hh = jnp.dot(h_hi, w2_ref[...], preferred_element_type=F32)
    hl = jnp.dot(h_lo, wh_ref[...], preferred_element_type=F32)
    logits = hh[:, :LANES] + hh[:, LANES:] + hl + rb_ref[...]
    idx, wts = _route(logits, n_groups, per_group)
    idx_ref[...] = idx
    wts_ref[...] = wts


def _mod_specs(lay, layer, shift_idx, scale_idx):
    d = lay.d
    return [pl.BlockSpec((None, 1, d), lambda i: (_mod_row(lay, layer, i), 0, shift_idx)),
            pl.BlockSpec((None, 1, d), lambda i: (_mod_row(lay, layer, i), 0, scale_idx))]


def _normmod(lay, tok, g, mods, layer, shift_idx, scale_idx):
    d, tm = lay.d, lay.tm
    return pl.pallas_call(
        _normmod_kernel,
        grid=(lay.row_tiles,),
        in_specs=[pl.BlockSpec((tm, d), lambda i: (i, 0)),
                  pl.BlockSpec((1, d), lambda i: (0, 0))] + _mod_specs(lay, layer, shift_idx, scale_idx),
        out_specs=pl.BlockSpec((tm, d), lambda i: (i, 0)),
        out_shape=jax.ShapeDtypeStruct((lay.n_tok, d), BF16),
        compiler_params=_params(1),
        name="normmod",
    )(tok, g.reshape(1, d), mods, mods)


def _normmod_router(lay, tok, g, mods, layer, shift_idx, scale_idx, wg_r, bg_r, we_r, be_r):
    d, tm = lay.d, lay.tm
    n_groups, per_group = we_r.shape[1], we_r.shape[2]
    n_exp = n_groups * per_group
    assert n_exp + n_groups <= LANES
    wr = jnp.concatenate([we_r.reshape(d, n_exp), wg_r], axis=1)
    wr = jnp.pad(wr, ((0, 0), (0, LANES - wr.shape[1])))
    rb = jnp.pad(jnp.concatenate([be_r.reshape(n_exp), bg_r]), (0, LANES - n_exp - n_groups)).reshape(1, LANES)
    w_hi = wr.astype(BF16)
    w_lo = (wr - w_hi.astype(F32)).astype(BF16)
    w2 = jnp.concatenate([w_hi, w_lo], axis=1)
    return pl.pallas_call(
        functools.partial(_normmod_router_kernel, n_groups=n_groups, per_group=per_group),
        grid=(lay.row_tiles,),
        in_specs=[pl.BlockSpec((tm, d), lambda i: (i, 0)),
                  pl.BlockSpec((1, d), lambda i: (0, 0))] + _mod_specs(lay, layer, shift_idx, scale_idx) + [
                  pl.BlockSpec((d, 2 * LANES), lambda i: (0, 0)),
                  pl.BlockSpec((d, LANES), lambda i: (0, 0)),
                  pl.BlockSpec((1, LANES), lambda i: (0, 0))],
        out_specs=[pl.BlockSpec((tm, d // 2), lambda i: (i, 0)),
                   pl.BlockSpec((tm, LANES), lambda i: (i, 0)),
                   pl.BlockSpec((tm, LANES), lambda i: (i, 0))],
        out_shape=[jax.ShapeDtypeStruct((lay.n_tok, d // 2), jnp.uint32),
                   jax.ShapeDtypeStruct((lay.n_tok, LANES), jnp.int32),
                   jax.ShapeDtypeStruct((lay.n_tok, LANES), F32)],
        compiler_params=_params(1),
        name="normmod_router",
    )(tok, g.reshape(1, d), mods, mods, w2, w_hi, rb)


def _final_norm_kernel(t_ref, g_ref, o_ref):
    o_ref[...] = _rms(t_ref[...], g_ref[...])


def _final_norm(lay, tok, g):
    d, tm = lay.d, lay.tm
    return pl.pallas_call(
        _final_norm_kernel,
        grid=(lay.lat_tiles,),
        in_specs=[pl.BlockSpec((tm, d), lambda i: (i, 0)),
                  pl.BlockSpec((1, d), lambda i: (0, 0))],
        out_specs=pl.BlockSpec((tm, d), lambda i: (i, 0)),
        out_shape=jax.ShapeDtypeStruct((lay.n_lat, d), F32),
        compiler_params=_params(1),
        name="final_norm",
    )(tok, g.reshape(1, d))


def _fused_matmul_kernel(*refs, n_a, n_w, n_extra, n_out, epilogue, lat_tiles):
    a_refs = refs[:n_a]
    w_refs = refs[n_a:n_a + n_w]
    e_refs = refs[n_a + n_w:n_a + n_w + n_extra]
    o_refs = refs[n_a + n_w + n_extra:n_a + n_w + n_extra + n_out]
    wb_refs = refs[n_a + n_w + n_extra + n_out:]

    @pl.when(pl.program_id(1) == 0)
    def _():
        for w_ref, wb_ref in zip(w_refs, wb_refs):
            wb_ref[...] = w_ref[...].astype(BF16)

    def run(a_ref):
        a = a_ref[...]
        accs = [jnp.dot(a, wb_ref[...], preferred_element_type=F32) for wb_ref in wb_refs]
        epilogue(accs, e_refs, o_refs)

    if n_a == 1:
        run(a_refs[0])
    else:
        is_lat = pl.program_id(1) < lat_tiles
        pl.when(is_lat)(lambda: run(a_refs[0]))
        pl.when(jnp.logical_not(is_lat))(lambda: run(a_refs[1]))


def _fused_matmul(a, w_specs, extras, outs, epilogue, *, tm, tn, n_col_blocks, name):
    if isinstance(a, tuple):
        a_lat, a_ctx = a
        k = a_lat.shape[1]
        lat_tiles = a_lat.shape[0] // tm
        assert a_ctx.shape == (tm, k)
        a_args = [a_lat, a_ctx]
        a_specs = [pl.BlockSpec((tm, k), lambda j, i: (jnp.minimum(i, lat_tiles - 1), 0)),
                   pl.BlockSpec((tm, k), lambda j, i: (0, 0))]
        row_tiles = lat_tiles + 1
    else:
        m, k = a.shape
        lat_tiles = None
        a_args = [a]
        a_specs = [pl.BlockSpec((tm, k), lambda j, i: (i, 0))]
        row_tiles = m // tm
    kernel = functools.partial(_fused_matmul_kernel, n_a=len(a_args), n_w=len(w_specs), n_extra=len(extras),
                               n_out=len(outs), epilogue=epilogue, lat_tiles=lat_tiles)
    res = pl.pallas_call(
        kernel,
        grid=(n_col_blocks, row_tiles),
        in_specs=a_specs + [s for _, s in w_specs] + [s for _, s in extras],
        out_specs=[s for _, s in outs],
        out_shape=[s for s, _ in outs],
        scratch_shapes=[pltpu.VMEM((k, tn), BF16) for _ in w_specs],
        compiler_params=_params(2),
        name=name,
    )(*a_args, *[w for w, _ in w_specs], *[e for e, _ in extras])
    return res


def _w2d(w, tn):
    if isinstance(w, tuple):
        stack, slot = w
        return (stack, pl.BlockSpec((None, stack.shape[1], tn), lambda j, i: (slot, 0, j)))
    return (w, pl.BlockSpec((w.shape[0], tn), lambda j, i: (0, j)))


def _tile_spec(tm, tn):
    return pl.BlockSpec((tm, tn), lambda j, i: (i, j))


def _row_spec(tn):
    return pl.BlockSpec((1, tn), lambda j, i: (0, j))


def _ep_store(dtype):
    def ep(accs, e_refs, o_refs):
        o_refs[0][...] = accs[0].astype(dtype)
    return ep


def _linear(a, w, *, tm, tn, dtype, name):
    n = w[0].shape[2] if isinstance(w, tuple) else w.shape[1]
    return _fused_matmul(a, [_w2d(w, tn)], [], [(jax.ShapeDtypeStruct((a.shape[0], n), dtype), _tile_spec(tm, tn))],
                         _ep_store(dtype), tm=tm, tn=tn, n_col_blocks=n // tn, name=name)[0]


def _ep_residual(accs, e_refs, o_refs):
    tok_ref, gate_ref = e_refs
    o_refs[0][...] = tok_ref[...] + gate_ref[...] * accs[0]


def _ep_residual_bias(accs, e_refs, o_refs):
    tok_ref, gate_ref, b_ref = e_refs
    o_refs[0][...] = tok_ref[...] + gate_ref[...] * (accs[0] + b_ref[...])


def _residual_linear(lay, a, w, tok, mods, layer, gate_idx, *, tn, bias=None, name):
    d, tm = lay.d, lay.tm
    cols = d // tn
    extras = [(tok, _tile_spec(tm, tn)),
              (mods, pl.BlockSpec((None, 1, tn), lambda j, i: (_mod_row(lay, layer, i), 0, gate_idx * cols + j)))]
    ep = _ep_residual
    if bias is not None:
        extras.append((bias.reshape(1, d), _row_spec(tn)))
        ep = _ep_residual_bias
    return _fused_matmul(a, [_w2d(w, tn)], extras, [(jax.ShapeDtypeStruct((lay.n_tok, d), F32), _tile_spec(tm, tn))],
                         ep, tm=tm, tn=tn, n_col_blocks=cols, name=name)[0]


def _flat_positions_table(lay, lat_table, ctx_row):
    lat = jnp.tile(lat_table, (lay.batch, 1))
    ctx = jnp.broadcast_to(ctx_row, (lay.batch * lay.ctx, lat_table.shape[1]))
    return jnp.concatenate([lat, ctx], axis=0)


def _mla_rope_tables(lay, rope_dim):
    rows = lay.seq // GRID_W
    grid = jnp.stack(jnp.meshgrid(jnp.arange(rows), jnp.arange(GRID_W), indexing='ij'), axis=-1)
    grid = grid.reshape(-1, 2).astype(F32)
    n_freq = rope_dim // 4
    inv = ROPE_BASE ** (-jnp.arange(n_freq, dtype=F32) / n_freq)
    ang = jnp.concatenate([grid[:, :1] * inv, grid[:, 1:] * inv], axis=-1)
    cos, sin = jnp.cos(ang), jnp.sin(ang)
    pad = LANES - rope_dim
    cos_t = jnp.concatenate([cos, cos, jnp.ones((lay.seq, pad), F32)], axis=-1)
    sin_t = jnp.concatenate([-sin, sin, jnp.zeros((lay.seq, pad), F32)], axis=-1)
    one = jnp.ones((1, LANES), F32)
    return _flat_positions_table(lay, cos_t, one), _flat_positions_table(lay, sin_t, 0.0 * one)


def _rope_slab(r, cos_t, sin_t, half):
    lane = lax.broadcasted_iota(jnp.int32, r.shape, 1)
    partner = jnp.where(lane < half, pltpu.roll(r, LANES - half, 1), pltpu.roll(r, half, 1))
    return r * cos_t + partner * sin_t


def _ret_tables(lay, dk):
    theta = 1.0 / (RET_THETA_BASE ** jnp.linspace(0.0, 1.0, dk // 2, dtype=F32))
    ang = jnp.arange(lay.seq, dtype=F32)[:, None] * theta
    one = jnp.ones((1, dk // 2), F32)
    return (_flat_positions_table(lay, jnp.cos(ang), one),
            _flat_positions_table(lay, jnp.sin(ang), 0.0 * one))


def _mla_norm_kernel(cq_ref, kv_ref, qn_ref, kvn_ref, cos_ref, sin_ref, cqn_ref, ckv_ref, kr_ref, *, kv_lora, rope):
    cqn_ref[...] = _rms(cq_ref[...], qn_ref[...]).astype(BF16)
    kv = kv_ref[...]
    ckv_ref[...] = _rms(kv[:, :kv_lora], kvn_ref[...]).astype(BF16)
    kr_ref[...] = _rope_slab(kv[:, kv_lora:], cos_ref[...], sin_ref[...], rope // 2).astype(BF16)


def _mla_norm(lay, cq, kv, q_norm, kv_norm, cos_t, sin_t, kv_lora, rope):
    tm = lay.tm
    q_lora = cq.shape[1]
    row = lambda w: pl.BlockSpec((tm, w), lambda i: (i, 0))
    const = lambda w: pl.BlockSpec((1, w), lambda i: (0, 0))
    return pl.pallas_call(
        functools.partial(_mla_norm_kernel, kv_lora=kv_lora, rope=rope),
        grid=(lay.row_tiles,),
        in_specs=[row(q_lora), row(kv_lora + LANES), const(q_lora), const(kv_lora), row(LANES), row(LANES)],
        out_specs=[row(q_lora), row(kv_lora), row(LANES)],
        out_shape=[jax.ShapeDtypeStruct((lay.n_tok, q_lora), BF16),
                   jax.ShapeDtypeStruct((lay.n_tok, kv_lora), BF16),
                   jax.ShapeDtypeStruct((lay.n_tok, LANES), BF16)],
        compiler_params=_params(1),
        name="mla_norm",
    )(cq, kv, q_norm.reshape(1, q_lora), kv_norm.reshape(1, kv_lora), cos_t, sin_t)


def _ep_mla_q(accs, e_refs, o_refs, *, heads_per_tile, rope):
    cos_ref, sin_ref = e_refs
    acc = accs[0]
    cos_t, sin_t = cos_ref[...], sin_ref[...]
    for h in range(heads_per_tile):
        base = h * 2 * LANES
        o_refs[0][:, base:base + LANES] = acc[:, base:base + LANES].astype(BF16)
        slab = _rope_slab(acc[:, base + LANES:base + 2 * LANES], cos_t, sin_t, rope // 2)
        o_refs[0][:, base + LANES:base + 2 * LANES] = slab.astype(BF16)


def _ep_mla_k(accs, e_refs, o_refs, *, heads_per_tile):
    kr = e_refs[0][...]
    acc = accs[0]
    for h in range(heads_per_tile):
        o_refs[0][:, h * 2 * LANES:h * 2 * LANES + LANES] = acc[:, h * LANES:(h + 1) * LANES].astype(BF16)
        o_refs[0][:, h * 2 * LANES + LANES:(h + 1) * 2 * LANES] = kr


def _attn_scores(q, kv_refs, chunks, s_ref):
    off = 0
    for ref_idx, start, size in chunks:
        k = kv_refs[2 * ref_idx][start:start + size, :]
        s_ref[:, off:off + size] = lax.dot_general(q, k, (((1,), (1,)), ((), ())), preferred_element_type=F32)
        off += size


ATTN_ROW_BLOCK = 64


def _attn_probs(s_ref, p_ref, l_ref, c):
    tq, n_keys = s_ref.shape
    rb = min(ATTN_ROW_BLOCK, tq)
    for r0 in range(0, tq, rb):
        m_acc = s_ref[r0:r0 + rb, 0:LANES]
        for j in range(1, n_keys // LANES):
            m_acc = jnp.maximum(m_acc, s_ref[r0:r0 + rb, j * LANES:(j + 1) * LANES])
        mc = jnp.broadcast_to(jnp.max(m_acc, axis=1, keepdims=True) * c, (rb, LANES))
        l_acc = jnp.zeros((rb, LANES), F32)
        for j in range(n_keys // LANES):
            p = jnp.exp2(s_ref[r0:r0 + rb, j * LANES:(j + 1) * LANES] * c - mc)
            l_acc = l_acc + p
            p_ref[r0:r0 + rb, j * LANES:(j + 1) * LANES] = p.astype(BF16)
        l_ref[r0:r0 + rb, :] = jnp.broadcast_to(jnp.sum(l_acc, axis=1, keepdims=True), (rb, LANES))


def _attn_output(p_ref, l_ref, kv_refs, n_kv):
    acc = None
    off = 0
    for ref_idx in range(n_kv):
        v_ref = kv_refs[2 * ref_idx + 1]
        n = v_ref.shape[0]
        part = jnp.dot(p_ref[:, off:off + n], v_ref[...], preferred_element_type=F32)
        acc = part if acc is None else acc + part
        off += n
    return acc / l_ref[...]


def _attn_kernel(q_ref, *refs, chunks, n_kv, scale, tq):
    kv_refs = refs[:2 * n_kv]
    o_ref = refs[2 * n_kv]
    scratch = refs[2 * n_kv + 1:]
    s_refs, p_refs, l_refs = scratch[0:2], scratch[2:4], scratch[4:6]
    assert o_ref.shape[1] == LANES
    nq = q_ref.shape[0] // tq
    c = scale * LOG2E

    def rows(i):
        return slice(i * tq, (i + 1) * tq)

    _attn_scores(q_ref[rows(0), :], kv_refs, chunks, s_refs[0])
    for i in range(nq):
        cur, nxt = i % 2, (i + 1) % 2
        if i >= 1:
            o_ref[rows(i - 1), :] = _attn_output(p_refs[nxt], l_refs[nxt], kv_refs, n_kv).astype(o_ref.dtype)
        if i + 1 < nq:
            _attn_scores(q_ref[rows(i + 1), :], kv_refs, chunks, s_refs[nxt])
        _attn_probs(s_refs[cur], p_refs[cur], l_refs[cur], c)
    last = (nq - 1) % 2
    o_ref[rows(nq - 1), :] = _attn_output(p_refs[last], l_refs[last], kv_refs, n_kv).astype(o_ref.dtype)


def _attn_scratch(tq, n_keys):
    return ([pltpu.VMEM((tq, n_keys), F32)] * 2 + [pltpu.VMEM((tq, n_keys), BF16)] * 2
            + [pltpu.VMEM((tq, LANES), F32)] * 2)


def _mla_attention(lay, q, k, v, heads, scale, tq, tk):
    b, s, c = lay.batch, lay.seq, lay.ctx
    qk_w, v_w = 2 * LANES, LANES
    ctx_blk0 = lay.n_lat // c
    lat_chunks = tuple([(0, 0, c)] + [(1, st, tk) for st in range(0, s, tk)])
    assert s % tq == 0 and s % tk == 0
    o_lat = pl.pallas_call(
        functools.partial(_attn_kernel, chunks=lat_chunks, n_kv=2, scale=scale, tq=tq),
        scratch_shapes=_attn_scratch(tq, c + s),
        grid=(b, heads),
        in_specs=[pl.BlockSpec((s, qk_w), lambda bi, h: (bi, h)),
                  pl.BlockSpec((c, qk_w), lambda bi, h: (ctx_blk0 + bi, h)),
                  pl.BlockSpec((c, v_w), lambda bi, h: (ctx_blk0 + bi, h)),
                  pl.BlockSpec((s, qk_w), lambda bi, h: (bi, h)),
                  pl.BlockSpec((s, v_w), lambda bi, h: (bi, h))],
        out_specs=pl.BlockSpec((s, v_w), lambda bi, h: (bi, h)),
        out_shape=jax.ShapeDtypeStruct((lay.n_lat, heads * v_w), BF16),
        compiler_params=_params(2),
        name="mla_attn_latent",
    )(q, k, v, k, v)
    o_ctx = pl.pallas_call(
        functools.partial(_attn_kernel, chunks=((0, 0, c),), n_kv=1, scale=scale, tq=c),
        scratch_shapes=_attn_scratch(c, c),
        grid=(b, heads),
        in_specs=[pl.BlockSpec((c, qk_w), lambda bi, h: (ctx_blk0 + bi, h)),
                  pl.BlockSpec((c, qk_w), lambda bi, h: (ctx_blk0 + bi, h)),
                  pl.BlockSpec((c, v_w), lambda bi, h: (ctx_blk0 + bi, h))],
        out_specs=pl.BlockSpec((c, v_w), lambda bi, h: (bi, h)),
        out_shape=jax.ShapeDtypeStruct((b * c, heads * v_w), BF16),
        compiler_params=_params(2),
        name="mla_attn_ctx",
    )(q, k, v)
    return o_lat, o_ctx


def _mla_mixer(lay, h, tok, mods, layer, slot, wq_a_all, q_norm, wq_b, wkv_a, kv_norm, wkv_b, wo_all):
    d, tm = lay.d, lay.tm
    q_lora = wq_a_all.shape[2]
    wq_a = (wq_a_all, slot)
    heads, qk = wq_b.shape[1], wq_b.shape[2]
    kv_lora = kv_norm.shape[0]
    rope = wkv_a.shape[1] - kv_lora
    nope = qk - rope
    v_dim = wkv_b.shape[2] - nope
    assert nope == LANES and v_dim == LANES and rope <= LANES and rope % 4 == 0
    cos_t, sin_t = _mla_rope_tables(lay, rope)

    cq = _linear(h, wq_a, tm=tm, tn=min(512, q_lora), dtype=F32, name="mla_q_a")
    wkv_a_pad = jnp.pad(wkv_a, ((0, 0), (0, LANES - rope)))
    kv = _linear(h, wkv_a_pad, tm=tm, tn=kv_lora + LANES, dtype=F32, name="mla_kv_a")
    cqn, ckv, kr = _mla_norm(lay, cq, kv, q_norm, kv_norm, cos_t, sin_t, kv_lora, rope)

    hpt = min(8, heads)
    wq = jnp.pad(wq_b, ((0, 0), (0, 0), (0, 2 * LANES - qk))).reshape(q_lora, heads * 2 * LANES)
    tn_q = hpt * 2 * LANES
    q = _fused_matmul(
        cqn, [_w2d(wq, tn_q)],
        [(cos_t, pl.BlockSpec((tm, LANES), lambda j, i: (i, 0))), (sin_t, pl.BlockSpec((tm, LANES), lambda j, i: (i, 0)))],
        [(jax.ShapeDtypeStruct((lay.n_tok, heads * 2 * LANES), BF16), _tile_spec(tm, tn_q))],
        functools.partial(_ep_mla_q, heads_per_tile=hpt, rope=rope),
        tm=tm, tn=tn_q, n_col_blocks=heads // hpt, name="mla_q_b")[0]

    wk = wkv_b[:, :, :nope].reshape(kv_lora, heads * nope)
    wv = wkv_b[:, :, nope:].reshape(kv_lora, heads * v_dim)
    hpt_k = min(16, heads)
    k = _fused_matmul(
        ckv, [_w2d(wk, hpt_k * LANES)],
        [(kr, pl.BlockSpec((tm, LANES), lambda j, i: (i, 0)))],
        [(jax.ShapeDtypeStruct((lay.n_tok, heads * 2 * LANES), BF16), _tile_spec(tm, hpt_k * 2 * LANES))],
        functools.partial(_ep_mla_k, heads_per_tile=hpt_k),
        tm=tm, tn=hpt_k * LANES, n_col_blocks=heads // hpt_k, name="mla_k_b")[0]
    v = _linear(ckv, wv, tm=tm, tn=min(2048, heads * v_dim), dtype=BF16, name="mla_v_b")

    o = _mla_attention(lay, q, k, v, heads, qk ** -0.5, tq=min(512, lay.seq // 2), tk=min(512, lay.seq))
    wo = (wo_all.reshape(wo_all.shape[0], heads * v_dim, d), slot)
    return _residual_linear(lay, o, wo, tok, mods, layer, 2, tn=min(512, d), name="mla_o")


def _ep_rotate(accs, e_refs, o_refs, *, heads_per_tile, dk, scale):
    cos_ref, sin_ref = e_refs
    cos, sin = cos_ref[...], sin_ref[...]
    acc = accs[0]
    half = dk // 2
    for h in range(heads_per_tile):
        x1 = acc[:, h * dk:h * dk + half]
        x2 = acc[:, h * dk + half:(h + 1) * dk]
        o_refs[0][:, h * dk:h * dk + half] = ((x1 * cos - x2 * sin) * scale).astype(BF16)
        o_refs[0][:, h * dk + half:(h + 1) * dk] = ((x1 * sin + x2 * cos) * scale).astype(BF16)


def _rotated_linear(lay, a, w, cos, sin, dk, scale, name):
    tm = lay.tm
    n = w.shape[1]
    tn = 2 * dk
    half = dk // 2
    return _fused_matmul(
        a, [_w2d(w, tn)],
        [(cos, pl.BlockSpec((tm, half), lambda j, i: (i, 0))), (sin, pl.BlockSpec((tm, half), lambda j, i: (i, 0)))],
        [(jax.ShapeDtypeStruct((lay.n_tok, n), BF16), _tile_spec(tm, tn))],
        functools.partial(_ep_rotate, heads_per_tile=tn // dk, dk=dk, scale=scale),
        tm=tm, tn=tn, n_col_blocks=n // tn, name=name)[0]


def _retention_kernel(lg_ref, ql_ref, kl_ref, vl_ref, qc_ref, kc_ref, vc_ref, yl_ref, yc_ref,
                      sf_ref, sb_ref, yfl_ref, yfc_ref, ybl_ref, ybc_ref, *, chunk, heads):
    h = pl.program_id(1)
    lg_f = lg_ref[h]
    lg_b = lg_ref[heads + h]
    n_lat = ql_ref.shape[0] // chunk
    n_ctx = qc_ref.shape[0] // chunk
    dv = vl_ref.shape[1]

    row = lax.broadcasted_iota(jnp.int32, (chunk, chunk), 0).astype(F32)
    col = lax.broadcasted_iota(jnp.int32, (chunk, chunk), 1).astype(F32)
    pos = lax.broadcasted_iota(jnp.int32, (chunk, dv), 0).astype(F32)
    diff = row - col
    dmat_f = jnp.where(diff >= 0, jnp.exp(jnp.where(diff >= 0, diff, 0.0) * lg_f), 0.0)
    dmat_b = jnp.where(diff <= 0, jnp.exp(jnp.where(diff <= 0, -diff, 0.0) * lg_b), 0.0)
    qdec_f = jnp.exp((pos + 1.0) * lg_f)
    kdec_f = jnp.exp((chunk - 1.0 - pos) * lg_f)
    qdec_b = jnp.exp((chunk - pos) * lg_b)
    kdec_b = jnp.exp(pos * lg_b)
    blk_f = jnp.exp(chunk * lg_f)
    blk_b = jnp.exp(chunk * lg_b)

    def step(q, k, v, s_ref, dmat, qdec, kdec, blk):
        scores = lax.dot_general(q, k, (((1,), (1,)), ((), ())), preferred_element_type=F32) * dmat
        inner = jnp.dot(scores.astype(BF16), v, preferred_element_type=F32)
        state = s_ref[...]
        cross = jnp.dot(q, state.astype(BF16), preferred_element_type=F32) * qdec
        kd = (k.astype(F32) * kdec).astype(BF16)
        s_ref[...] = state * blk + lax.dot_general(kd, v, (((0,), (0,)), ((), ())), preferred_element_type=F32)
        return inner + cross

    def fwd(q_ref, k_ref, v_ref, y_ref, t):
        sl = pl.ds(pl.multiple_of(t * chunk, chunk), chunk)
        y_ref[sl, :] = step(q_ref[sl, :], k_ref[sl, :], v_ref[sl, :], sf_ref, dmat_f, qdec_f, kdec_f, blk_f)

    def bwd(q_ref, k_ref, v_ref, y_ref, t):
        sl = pl.ds(pl.multiple_of(t * chunk, chunk), chunk)
        y_ref[sl, :] = step(q_ref[sl, :], k_ref[sl, :], v_ref[sl, :], sb_ref, dmat_b, qdec_b, kdec_b, blk_b)

    sf_ref[...] = jnp.zeros_like(sf_ref)
    sb_ref[...] = jnp.zeros_like(sb_ref)
    for t in range(n_ctx):
        fwd(qc_ref, kc_ref, vc_ref, yfc_ref, t)
        bwd(qc_ref, kc_ref, vc_ref, ybc_ref, n_ctx - 1 - t)

    def body(t, carry):
        fwd(ql_ref, kl_ref, vl_ref, yfl_ref, t)
        bwd(ql_ref, kl_ref, vl_ref, ybl_ref, n_lat - 1 - t)
        return carry

    lax.fori_loop(0, n_lat, body, 0)

    def group_norm(y):
        mu = jnp.mean(y, axis=-1, keepdims=True)
        var = jnp.mean(jnp.square(y - mu), axis=-1, keepdims=True)
        return (y - mu) * lax.rsqrt(var + GROUP_NORM_EPS)

    for t in range(n_ctx):
        sl = pl.ds(t * chunk, chunk)
        yc_ref[sl, :] = group_norm(yfc_ref[sl, :] + ybc_ref[sl, :]).astype(BF16)

    def norm_body(t, carry):
        sl = pl.ds(pl.multiple_of(t * chunk, chunk), chunk)
        yl_ref[sl, :] = group_norm(yfl_ref[sl, :] + ybl_ref[sl, :]).astype(BF16)
        return carry

    lax.fori_loop(0, n_lat, norm_body, 0, unroll=4)


def _retention(lay, q, k, v, log_g, heads, dk, dv, chunk):
    b, s, c = lay.batch, lay.seq, lay.ctx
    ctx_blk0 = lay.n_lat // c
    lat = lambda w: pl.BlockSpec((s, w), lambda bi, h, lg: (bi, h))
    ctx = lambda w: pl.BlockSpec((c, w), lambda bi, h, lg: (ctx_blk0 + bi, h))
    y_lat, y_ctx = pl.pallas_call(
        functools.partial(_retention_kernel, chunk=chunk, heads=heads),
        grid_spec=pltpu.PrefetchScalarGridSpec(
            num_scalar_prefetch=1,
            grid=(b, heads),
            in_specs=[lat(dk), lat(dk), lat(dv), ctx(dk), ctx(dk), ctx(dv)],
            out_specs=[lat(dv), pl.BlockSpec((c, dv), lambda bi, h, lg: (bi, h))],
            scratch_shapes=[pltpu.VMEM((dk, dv), F32), pltpu.VMEM((dk, dv), F32),
                            pltpu.VMEM((s, dv), F32), pltpu.VMEM((c, dv), F32),
                            pltpu.VMEM((s, dv), F32), pltpu.VMEM((c, dv), F32)]),
        out_shape=[jax.ShapeDtypeStruct((lay.n_lat, heads * dv), BF16),
                   jax.ShapeDtypeStruct((b * c, heads * dv), BF16)],
        compiler_params=_params(2),
        name="retention",
    )(log_g.reshape(-1), q, k, v, q, k, v)
    return y_lat, y_ctx


def _ep_gate_mul(accs, e_refs, o_refs, *, lat_tiles):
    y_lat_ref, y_ctx_ref = e_refs
    y = jnp.where(pl.program_id(1) < lat_tiles, y_lat_ref[...], y_ctx_ref[...])
    o_refs[0][...] = (_silu(accs[0]) * y.astype(F32)).astype(BF16)


def _retention_mixer(lay, h, tok, mods, layer, wq, wk, wv, wg, wo, decay):
    d, tm = lay.d, lay.tm
    heads, dk = wq.shape[1], wq.shape[2]
    dv = wv.shape[2]
    assert dk == dv and dk % (2 * LANES) == 0
    chunk = min(256, lay.ctx)
    assert lay.ctx % chunk == 0 and lay.seq % chunk == 0
    log_g = -jnp.exp(decay.astype(F32))
    cos, sin = _ret_tables(lay, dk)
    q = _rotated_linear(lay, h, wq.reshape(d, heads * dk), cos, sin, dk, 1.0, "ret_q")
    k = _rotated_linear(lay, h, wk.reshape(d, heads * dk), cos, sin, dk, dk ** -0.5, "ret_k")
    v = _linear(h, wv.reshape(d, heads * dv), tm=tm, tn=min(512, heads * dv), dtype=BF16, name="ret_v")
    y_lat, y_ctx = _retention(lay, q, k, v, log_g, heads, dk, dv, chunk)
    tn = min(512, heads * dv)
    lat_tiles = lay.lat_tiles
    z = _fused_matmul(h, [_w2d(wg, tn)],
                      [(y_lat, pl.BlockSpec((tm, tn), lambda j, i: (jnp.minimum(i, lat_tiles - 1), j))),
                       (y_ctx, pl.BlockSpec((tm, tn), lambda j, i: (0, j)))],
                      [(jax.ShapeDtypeStruct((lay.n_tok, heads * dv), BF16), _tile_spec(tm, tn))],
                      functools.partial(_ep_gate_mul, lat_tiles=lat_tiles),
                      tm=tm, tn=tn, n_col_blocks=heads * dv // tn, name="ret_gate")[0]
    return _residual_linear(lay, z, wo, tok, mods, layer, 2, tn=min(512, d), name="ret_o")


def _ep_glu(accs, e_refs, o_refs):
    ba_ref, bg_ref = e_refs
    o_refs[0][...] = (accs[0] + ba_ref[...]) * jax.nn.sigmoid(accs[1] + bg_ref[...])


HALO = 16


SUBLANES = 8


def _dwconv_kernel(prev_ref, cur_ref, next_ref, w_ref, wb_ref, g_ref, b_ref, o_ref, buf_ref, acc_ref, *,
                   width, tt, tiles_per_seq, tiles_per_ctx, lat_tiles):
    i = pl.program_id(0)
    in_lat = i < lat_tiles
    pos = jnp.where(in_lat, i % tiles_per_seq, (i - lat_tiles) % tiles_per_ctx)
    n_seq_tiles = jnp.where(in_lat, tiles_per_seq, tiles_per_ctx)
    first = pos == 0
    last = pos == n_seq_tiles - 1
    buf_ref[0:HALO, :] = jnp.where(first, 0.0, prev_ref[...])
    buf_ref[HALO:HALO + tt, :] = cur_ref[...]
    buf_ref[HALO + tt:HALO + tt + HALO, :] = jnp.where(last, 0.0, next_ref[...])
    lead = HALO - width // 2
    n_out = tt // SUBLANES
    max_dblk = (lead + width - 1) // SUBLANES
    assert (n_out + max_dblk + 1) * SUBLANES <= tt + 2 * HALO
    sub = lax.broadcasted_iota(jnp.int32, (SUBLANES, LANES), 0)

    def lane_chunk(cidx, carry):
        ls = pl.ds(pl.multiple_of(cidx * LANES, LANES), LANES)
        bias = jnp.broadcast_to(wb_ref[:, ls], (SUBLANES, LANES))
        taps = [jnp.broadcast_to(w_ref[k:k + 1, ls], (SUBLANES, LANES)) for k in range(width)]

        def load(blk):
            v = buf_ref[blk * SUBLANES:(blk + 1) * SUBLANES, ls]
            return v, {s: pltpu.roll(v, SUBLANES - s, 0) for s in range(1, SUBLANES)}

        accs = {}
        nxt = load(0)
        for blk in range(n_out + max_dblk):
            (va, rolls_a), nxt = nxt, load(blk + 1)
            for s in range(SUBLANES):
                x = va if s == 0 else jnp.where(sub < SUBLANES - s, rolls_a[s], nxt[1][s])
                for dblk in range(max_dblk + 1):
                    k = SUBLANES * dblk + s - lead
                    out_blk = blk - dblk
                    if 0 <= k < width and 0 <= out_blk < n_out:
                        accs[out_blk] = accs.get(out_blk, bias) + x * taps[k]
            done = blk - max_dblk
            if done >= 0:
                acc_ref[done * SUBLANES:(done + 1) * SUBLANES, ls] = accs.pop(done)
        assert not accs
        return carry

    lax.fori_loop(0, cur_ref.shape[1] // LANES, lane_chunk, 0)

    ln_rows = 2 * SUBLANES

    def ln_block(rb, carry):
        rs = pl.ds(pl.multiple_of(rb * ln_rows, ln_rows), ln_rows)
        acc = acc_ref[rs, :]
        mu = jnp.mean(acc, axis=-1, keepdims=True)
        cen = acc - mu
        var = jnp.mean(cen * cen, axis=-1, keepdims=True)
        y = cen * lax.rsqrt(var + NORM_EPS) * g_ref[...] + b_ref[...]
        o_ref[rs, :] = _silu(y).astype(BF16)
        return carry

    lax.fori_loop(0, tt // ln_rows, ln_block, 0, unroll=4)


def _dwconv_ln_swish(lay, u, dw, dw_b, ln_g, ln_b):
    d = lay.d
    width = dw.shape[0]
    assert width // 2 <= HALO
    tt = min(256, lay.ctx)
    assert lay.ctx % tt == 0 and lay.seq % tt == 0 and tt % HALO == 0
    r = tt // HALO
    n_tiles = lay.n_tok // tt
    n_halo_blocks = lay.n_tok // HALO
    const = lambda rows: pl.BlockSpec((rows, d), lambda i: (0, 0))
    kern = functools.partial(_dwconv_kernel, width=width, tt=tt, tiles_per_seq=lay.seq // tt,
                             tiles_per_ctx=lay.ctx // tt, lat_tiles=lay.n_lat // tt)
    return pl.pallas_call(
        kern,
        grid=(n_tiles,),
        in_specs=[pl.BlockSpec((HALO, d), lambda i: (jnp.maximum(i * r - 1, 0), 0)),
                  pl.BlockSpec((tt, d), lambda i: (i, 0)),
                  pl.BlockSpec((HALO, d), lambda i: (jnp.minimum((i + 1) * r, n_halo_blocks - 1), 0)),
                  const(width), const(1), const(1), const(1)],
        out_specs=pl.BlockSpec((tt, d), lambda i: (i, 0)),
        out_shape=jax.ShapeDtypeStruct((lay.n_tok, d), BF16),
        scratch_shapes=[pltpu.VMEM((tt + 2 * HALO, d), F32), pltpu.VMEM((tt, d), F32)],
        compiler_params=_params(1),
        name="dwconv_ln_swish",
    )(u, u, u, dw, dw_b.reshape(1, d), ln_g.reshape(1, d), ln_b.reshape(1, d))


def _conformer_mixer(lay, h, tok, mods, layer, pw1, b1, dw, dw_b, ln_g, ln_b, pw2, b2):
    d, tm = lay.d, lay.tm
    tn = min(256, d)
    cols = d // tn
    b1r = b1.reshape(1, 2 * d)
    u = _fused_matmul(
        h,
        [(pw1, pl.BlockSpec((d, tn), lambda j, i: (0, j))), (pw1, pl.BlockSpec((d, tn), lambda j, i: (0, cols + j)))],
        [(b1r, pl.BlockSpec((1, tn), lambda j, i: (0, j))), (b1r, pl.BlockSpec((1, tn), lambda j, i: (0, cols + j)))],
        [(jax.ShapeDtypeStruct((lay.n_tok, d), F32), _tile_spec(tm, tn))],
        _ep_glu, tm=tm, tn=tn, n_col_blocks=cols, name="conv_pw1_glu")[0]
    a = _dwconv_ln_swish(lay, u, dw, dw_b, ln_g, ln_b)
    return _residual_linear(lay, a, pw2, tok, mods, layer, 2, tn=min(512, d), bias=b2, name="conv_pw2")


MOE_ROW_TILE = 256


def _dispatch_plan(idx, wts, n_exp, tmx):
    eid = idx[:, :MOE_TOP_K].reshape(-1)
    w = wts[:, :MOE_TOP_K].reshape(-1)
    n_pairs = eid.shape[0]
    onehot = (eid[:, None] == jnp.arange(n_exp, dtype=jnp.int32)[None, :]).astype(jnp.int32)
    csum = jnp.cumsum(onehot, axis=0)
    rank = jnp.sum((csum - onehot) * onehot, axis=1)
    counts = csum[-1]
    padded = ((counts + tmx - 1) // tmx) * tmx
    ends = jnp.cumsum(padded)
    starts = ends - padded
    pos = (starts[eid] + rank).astype(jnp.int32)
    n_tiles = (n_pairs + n_exp * (tmx - 1)) // tmx + 1
    n_slots = n_tiles * tmx
    n_used = (ends[-1] // tmx).astype(jnp.int32).reshape(1)
    tile_start = jnp.arange(n_tiles, dtype=jnp.int32) * tmx
    tile_expert = jnp.sum((tile_start[:, None] >= ends[None, :]).astype(jnp.int32), axis=1)
    tile_expert = jnp.minimum(tile_expert, n_exp - 1).astype(jnp.int32)
    slot_pair1 = jnp.zeros((n_slots,), jnp.int32).at[pos].set(jnp.arange(1, n_pairs + 1, dtype=jnp.int32),
                                                              unique_indices=True)
    slot_pair = jnp.maximum(slot_pair1 - 1, 0)
    src_token = slot_pair // MOE_TOP_K
    slot_w = jnp.where(slot_pair1 > 0, w[slot_pair], 0.0)
    pos_by_choice = pos.reshape(-1, MOE_TOP_K).T.reshape(-1)
    return pos_by_choice, src_token, slot_w.reshape(n_slots, 1), tile_expert, n_used


def _row_gather(src_hbm, dst, sem, index_of_row, n_rows, unroll=8):
    def body(r, carry):
        pltpu.make_async_copy(src_hbm.at[pl.ds(index_of_row(r), 1), :], dst.at[pl.ds(r, 1), :], sem).start()
        return carry
    lax.fori_loop(0, n_rows, body, 0, unroll=unroll)


def _row_gather_wait(src_hbm, dst, sem):
    pltpu.make_async_copy(src_hbm.at[pl.ds(0, dst.shape[0]), :], dst, sem).wait()


def _moe_up_kernel(te_ref, src_ref, nu_ref, hp_hbm, w1_ref, w3_ref, sw_ref, act_ref, buf, sem, w1b, w3b, *, tmx):
    t = pl.program_id(0)
    n_used = nu_ref[0]

    def start_tile(tile, slot, unroll):
        _row_gather(hp_hbm, buf.at[slot], sem.at[slot], lambda r: src_ref[tile * tmx + r], tmx, unroll)

    @pl.when(t == 0)
    def _():
        start_tile(0, 0, 8)

    @pl.when(t < n_used)
    def _():
        slot = t % 2
        new_expert = jnp.logical_or(t == 0, te_ref[t] != te_ref[jnp.maximum(t - 1, 0)])

        @pl.when(new_expert)
        def _():
            w1b[...] = w1_ref[...].astype(BF16)
            w3b[...] = w3_ref[...].astype(BF16)

        _row_gather_wait(hp_hbm, buf.at[slot], sem.at[slot])
        start_tile(jnp.minimum(t + 1, n_used - 1), 1 - slot, True)
        a_lo, a_hi = _unpack_bf16_pairs(buf[slot])
        half = a_lo.shape[1]
        u1 = (jnp.dot(a_lo, w1b[:half, :], preferred_element_type=F32)
              + jnp.dot(a_hi, w1b[half:, :], preferred_element_type=F32))
        u3 = (jnp.dot(a_lo, w3b[:half, :], preferred_element_type=F32)
              + jnp.dot(a_hi, w3b[half:, :], preferred_element_type=F32))
        act_ref[...] = (_silu(u1) * u3 * sw_ref[...]).astype(BF16)

    @pl.when(t == n_used)
    def _():
        _row_gather_wait(hp_hbm, buf.at[t % 2], sem.at[t % 2])

    @pl.when(t >= n_used)
    def _():
        act_ref[...] = jnp.zeros_like(act_ref)


def _moe_down_kernel(te_ref, nu_ref, act_ref, w2_ref, y_ref, w2b):
    t = pl.program_id(0)
    n_used = nu_ref[0]

    @pl.when(t < n_used)
    def _():
        new_expert = jnp.logical_or(t == 0, te_ref[t] != te_ref[jnp.maximum(t - 1, 0)])

        @pl.when(new_expert)
        def _():
            w2b[...] = w2_ref[...].astype(BF16)

        y_ref[...] = _pack_bf16_pairs(jnp.dot(act_ref[...], w2b[...], preferred_element_type=F32))

    @pl.when(t >= n_used)
    def _():
        y_ref[...] = jnp.zeros_like(y_ref)


def _moe_combine_kernel(pos_ref, y_hbm, tok_ref, gate_ref, *refs, tmc, n_steps, with_norm):
    if with_norm:
        g_ref, sh_ref, sc_ref, o_ref, h_ref, buf, sem = refs
    else:
        o_ref, buf, sem = refs
    i = pl.program_id(0)

    def start_tile(tile, slot):
        for choice in range(MOE_TOP_K):
            base = choice * n_steps * tmc + tile * tmc
            _row_gather(y_hbm, buf.at[slot, pl.ds(choice * tmc, tmc)], sem.at[slot],
                        lambda r, base=base: pos_ref[base + r], tmc)

    @pl.when(i == 0)
    def _():
        start_tile(0, 0)

    @pl.when(i + 1 < n_steps)
    def _():
        start_tile(i + 1, (i + 1) % 2)

    slot = i % 2
    _row_gather_wait(y_hbm, buf.at[slot], sem.at[slot])
    lo0, hi0 = _unpack_pairs_f32(buf[slot, 0:tmc, :])
    lo1, hi1 = _unpack_pairs_f32(buf[slot, tmc:2 * tmc, :])
    new = tok_ref[...] + gate_ref[...] * jnp.concatenate([lo0 + lo1, hi0 + hi1], axis=1)
    o_ref[...] = new
    if with_norm:
        h_ref[...] = (_rms(new, g_ref[...]) * (1.0 + sc_ref[...]) + sh_ref[...]).astype(BF16)


def _moe_routed(lay, hp, idx, wts, tok, mods, layer, w1, w3, w2, next_norm):
    d = lay.d
    n_exp, d_ff = w1.shape[1], w1.shape[3]
    tmx = MOE_ROW_TILE
    tmc = lay.tm // 2
    assert MOE_TOP_K == 2 and lay.n_tok % tmc == 0
    pos, src_token, slot_w, tile_expert, n_used = _dispatch_plan(idx, wts, n_exp, tmx)
    n_slots = src_token.shape[0]
    n_tiles = n_slots // tmx

    act = pl.pallas_call(
        functools.partial(_moe_up_kernel, tmx=tmx),
        grid_spec=pltpu.PrefetchScalarGridSpec(
            num_scalar_prefetch=3,
            grid=(n_tiles,),
            in_specs=[pl.BlockSpec(memory_space=pl.ANY),
                      pl.BlockSpec((None, None, d, d_ff), lambda t, te, src, nu: (layer, te[t], 0, 0)),
                      pl.BlockSpec((None, None, d, d_ff), lambda t, te, src, nu: (layer, te[t], 0, 0)),
                      pl.BlockSpec((tmx, 1), lambda t, te, src, nu: (t, 0))],
            out_specs=pl.BlockSpec((tmx, d_ff), lambda t, te, src, nu: (t, 0)),
            scratch_shapes=[pltpu.VMEM((2, tmx, d // 2), jnp.uint32), pltpu.SemaphoreType.DMA((2,)),
                            pltpu.VMEM((d, d_ff), BF16), pltpu.VMEM((d, d_ff), BF16)]),
        out_shape=jax.ShapeDtypeStruct((n_slots, d_ff), BF16),
        compiler_params=_params(1),
        name="moe_up",
    )(tile_expert, src_token, n_used, hp, w1, w3, slot_w)

    y = pl.pallas_call(
        _moe_down_kernel,
        grid_spec=pltpu.PrefetchScalarGridSpec(
            num_scalar_prefetch=2,
            grid=(n_tiles,),
            in_specs=[pl.BlockSpec((tmx, d_ff), lambda t, te, nu: (t, 0)),
                      pl.BlockSpec((None, None, d_ff, d), lambda t, te, nu: (layer, te[t], 0, 0))],
            out_specs=pl.BlockSpec((tmx, d // 2), lambda t, te, nu: (t, 0)),
            scratch_shapes=[pltpu.VMEM((d_ff, d), BF16)]),
        out_shape=jax.ShapeDtypeStruct((n_slots, d // 2), jnp.uint32),
        compiler_params=_params(1),
        name="moe_down",
    )(tile_expert, n_used, act, w2)

    n_steps = lay.n_tok // tmc
    gate_rows = lay.tm // tmc
    with_norm = next_norm is not None

    def mod_spec(mod_layer, which):
        return pl.BlockSpec((None, 1, d), lambda i, pos: (_mod_row(lay, mod_layer, i // gate_rows), 0, which))

    tile = pl.BlockSpec((tmc, d), lambda i, pos: (i, 0))
    in_specs = [pl.BlockSpec(memory_space=pl.ANY), tile, mod_spec(layer, 5)]
    args = [pos, y, tok, mods]
    out_specs, out_shape = [tile], [jax.ShapeDtypeStruct((lay.n_tok, d), F32)]
    if with_norm:
        in_specs += [pl.BlockSpec((1, d), lambda i, pos: (0, 0)), mod_spec(layer + 1, 0), mod_spec(layer + 1, 1)]
        args += [next_norm.reshape(1, d), mods, mods]
        out_specs.append(tile)
        out_shape.append(jax.ShapeDtypeStruct((lay.n_tok, d), BF16))
    res = pl.pallas_call(
        functools.partial(_moe_combine_kernel, tmc=tmc, n_steps=n_steps, with_norm=with_norm),
        grid_spec=pltpu.PrefetchScalarGridSpec(
            num_scalar_prefetch=1,
            grid=(n_steps,),
            in_specs=in_specs,
            out_specs=out_specs,
            scratch_shapes=[pltpu.VMEM((2, MOE_TOP_K * tmc, d // 2), jnp.uint32), pltpu.SemaphoreType.DMA((2,))]),
        out_shape=out_shape,
        compiler_params=_params(1),
        name="moe_combine",
    )(*args)
    return (res[0], res[1]) if with_norm else (res[0], None)


def kernel(x, c, ctx, c_ctx, ada_w, ada_b, norm_mix, norm_ffn, mla_wq_a, mla_q_norm, mla_wq_b, mla_wkv_a, mla_kv_norm, mla_wkv_b, mla_wo, ret_wq, ret_wk, ret_wv, ret_wg, ret_wo, ret_decay, conv_pw1, conv_b1, conv_dw, conv_dw_b, conv_ln_g, conv_ln_b, conv_pw2, conv_b2, moe_wg_router, moe_bg_router, moe_we_router, moe_be_router, moe_w1, moe_w3, moe_w2, final_norm):
    b, s, d = x.shape
    n_ctx = ctx.shape[1]
    depth = ada_w.shape[0]
    n_mixers = 3
    lay = Layout(batch=b, seq=s, ctx=n_ctx, d=d, tm=b * n_ctx)
    assert s % lay.tm == 0 and b + 1 <= MOD_ROWS and s % GRID_W == 0

    tok = jnp.concatenate([x.reshape(b * s, d), ctx.reshape(b * n_ctx, d)], axis=0)
    cond = jnp.zeros((MOD_ROWS, d), F32).at[:b].set(c).at[b].set(c_ctx)
    mods = _adaln(cond, ada_w, ada_b, tk=min(LANES, d)).reshape(depth * MOD_ROWS, 1, N_MOD * d)

    h = _normmod(lay, tok, norm_mix[0], mods, 0, 0, 1)
    for i in range(depth):
        kind, slot = i % n_mixers, i // n_mixers
        if kind == 0:
            tok = _mla_mixer(lay, h, tok, mods, i, slot, mla_wq_a, mla_q_norm[slot], mla_wq_b[slot], mla_wkv_a[slot],
                             mla_kv_norm[slot], mla_wkv_b[slot], mla_wo)
        elif kind == 1:
            tok = _retention_mixer(lay, h, tok, mods, i, ret_wq[slot], ret_wk[slot], ret_wv[slot], ret_wg[slot],
                                   ret_wo[slot], ret_decay[slot])
        else:
            tok = _conformer_mixer(lay, h, tok, mods, i, conv_pw1[slot], conv_b1[slot], conv_dw[slot], conv_dw_b[slot],
                                   conv_ln_g[slot], conv_ln_b[slot], conv_pw2[slot], conv_b2[slot])
        hp, idx, wts = _normmod_router(lay, tok, norm_ffn[i], mods, i, 3, 4, moe_wg_router[i], moe_bg_router[i],
                                       moe_we_router[i], moe_be_router[i])
        next_norm = norm_mix[i + 1] if i + 1 < depth else None
        tok, h = _moe_routed(lay, hp, idx, wts, tok, mods, i, moe_w1, moe_w3, moe_w2, next_norm)
    return _final_norm(lay, tok, final_norm).reshape(b, s, d)
```

```python
import functools
import math
from typing import NamedTuple

import jax
import jax.numpy as jnp
from jax import lax
from jax.experimental import pallas as pl
from jax.experimental.pallas import tpu as pltpu

F32 = jnp.float32
BF16 = jnp.bfloat16

GRID_W = 64
ROPE_BASE = 10000.0
RET_THETA_BASE = 10000.0
NORM_EPS = 1e-6
GROUP_NORM_EPS = 1e-5
N_MOD = 6
MOE_TOP_K = 2

LANES = 128
MOD_ROWS = 8
VMEM_LIMIT_BYTES = 56 * 1024 * 1024
LOG2E = math.log2(math.e)


class Layout(NamedTuple):
    batch: int
    seq: int
    ctx: int
    d: int
    tm: int

    @property
    def n_lat(self):
        return self.batch * self.seq

    @property
    def n_tok(self):
        return self.batch * (self.seq + self.ctx)

    @property
    def lat_tiles(self):
        return self.n_lat // self.tm

    @property
    def row_tiles(self):
        return self.n_tok // self.tm

    @property
    def tiles_per_batch(self):
        return self.seq // self.tm


def _params(n_axes):
    return pltpu.CompilerParams(dimension_semantics=("arbitrary",) * n_axes,
                                vmem_limit_bytes=VMEM_LIMIT_BYTES)


def _mod_row(lay, layer, i):
    return layer * MOD_ROWS + jnp.minimum(i // lay.tiles_per_batch, lay.batch)


def _silu(v):
    return v * jax.nn.sigmoid(v)


def _adaln_kernel(c_ref, w_ref, b_ref, o_ref):
    @pl.when(pl.program_id(1) == 0)
    def _():
        o_ref[...] = jnp.broadcast_to(b_ref[...], o_ref.shape)

    a = _silu(c_ref[...]).astype(BF16)
    o_ref[...] += jnp.dot(a, w_ref[...].astype(BF16), preferred_element_type=F32)


def _adaln(cond, ada_w, ada_b, tk):
    n_layers, d, n = ada_w.shape
    return pl.pallas_call(
        _adaln_kernel,
        grid=(n_layers, d // tk),
        in_specs=[pl.BlockSpec((MOD_ROWS, tk), lambda l, k: (0, k)),
                  pl.BlockSpec((None, tk, n), lambda l, k: (l, k, 0)),
                  pl.BlockSpec((None, 1, n), lambda l, k: (l, 0, 0))],
        out_specs=pl.BlockSpec((None, MOD_ROWS, n), lambda l, k: (l, 0, 0)),
        out_shape=jax.ShapeDtypeStruct((n_layers, MOD_ROWS, n), F32),
        compiler_params=_params(2),
        name="adaln",
    )(cond, ada_w, ada_b.reshape(n_layers, 1, n))


def _rms(x, g):
    return x * lax.rsqrt(jnp.mean(x * x, axis=-1, keepdims=True) + NORM_EPS) * g


def _normmod_kernel(t_ref, g_ref, sh_ref, sc_ref, h_ref):
    h = _rms(t_ref[...], g_ref[...]) * (1.0 + sc_ref[...]) + sh_ref[...]
    h_ref[...] = h.astype(BF16)


def _route(logits, n_groups, per_group):
    n_exp = n_groups * per_group
    lane = lax.broadcasted_iota(jnp.int32, logits.shape, 1)
    neg = jnp.float32(-jnp.inf)
    is_g = (lane >= n_exp) & (lane < n_exp + n_groups)
    lg = jnp.where(is_g, logits, neg)
    mg = jnp.max(lg, axis=1, keepdims=True)
    g_sel = jnp.min(jnp.where(lg == mg, lane, LANES), axis=1, keepdims=True) - n_exp
    pg_sel = 1.0 / jnp.sum(jnp.where(is_g, jnp.exp(lg - mg), 0.0), axis=1, keepdims=True)
    in_sel = (lane >= g_sel * per_group) & (lane < (g_sel + 1) * per_group)
    le = jnp.where(in_sel, logits, neg)
    m1 = jnp.max(le, axis=1, keepdims=True)
    i1 = jnp.min(jnp.where(le == m1, lane, LANES), axis=1, keepdims=True)
    le2 = jnp.where(lane == i1, neg, le)
    m2 = jnp.max(le2, axis=1, keepdims=True)
    i2 = jnp.min(jnp.where(le2 == m2, lane, LANES), axis=1, keepdims=True)
    e2 = jnp.exp(m2 - m1)
    w1 = pg_sel / (1.0 + e2)
    w2 = pg_sel * e2 / (1.0 + e2)
    idx = jnp.where(lane == 0, i1, jnp.where(lane == 1, i2, 0))
    wts = jnp.where(lane == 0, w1, jnp.where(lane == 1, w2, 0.0))
    return idx, wts


def _pack_bf16_pairs(h):
    half = h.shape[1] // 2
    hb = h.astype(BF16).astype(F32)
    lo = lax.shift_right_logical(lax.bitcast_convert_type(hb[:, :half], jnp.uint32), jnp.uint32(16))
    hi = lax.bitcast_convert_type(hb[:, half:], jnp.uint32) & jnp.uint32(0xFFFF0000)
    return hi | lo


def _unpack_pairs_f32(w):
    lo = lax.bitcast_convert_type(lax.shift_left(w, jnp.uint32(16)), F32)
    hi = lax.bitcast_convert_type(w & jnp.uint32(0xFFFF0000), F32)
    return lo, hi


def _unpack_bf16_pairs(w):
    lo, hi = _unpack_pairs_f32(w)
    return lo.astype(BF16), hi.astype(BF16)


def _normmod_router_kernel(t_ref, g_ref, sh_ref, sc_ref, w2_ref, wh_ref, rb_ref, hp_ref, idx_ref, wts_ref, *,
                           n_groups, per_group):
    h = _rms(t_ref[...], g_ref[...]) * (1.0 + sc_ref[...]) + sh_ref[...]
    hp_ref[...] = _pack_bf16_pairs(h)
    h_hi = h.astype(BF16)
    h_lo = (h - h_hi.astype(F32)).astype(BF16)
    hh = jnp.dot(h_hi, w2_ref[...], preferred_element_type=F32)
    hl = jnp.dot(h_lo, wh_ref[...], preferred_element_type=F32)
    logits = hh[:, :LANES] + hh[:, LANES:] + hl + rb_ref[...]
    idx, wts = _route(logits, n_groups, per_group)
    idx_ref[...] = idx
    wts_ref[...] = wts


def _mod_specs(lay, layer, shift_idx, scale_idx):
    d = lay.d
    return [pl.BlockSpec((None, 1, d), lambda i: (_mod_row(lay, layer, i), 0, shift_idx)),
            pl.BlockSpec((None, 1, d), lambda i: (_mod_row(lay, layer, i), 0, scale_idx))]


def _normmod(lay, tok, g, mods, layer, shift_idx, scale_idx):
    d, tm = lay.d, lay.tm
    return pl.pallas_call(
        _normmod_kernel,
        grid=(lay.row_tiles,),
        in_specs=[pl.BlockSpec((tm, d), lambda i: (i, 0)),
                  pl.BlockSpec((1, d), lambda i: (0, 0))] + _mod_specs(lay, layer, shift_idx, scale_idx),
        out_specs=pl.BlockSpec((tm, d), lambda i: (i, 0)),
        out_shape=jax.ShapeDtypeStruct((lay.n_tok, d), BF16),
        compiler_params=_params(1),
        name="normmod",
    )(tok, g.reshape(1, d), mods, mods)


def _normmod_router(lay, tok, g, mods, layer, shift_idx, scale_idx, wg_r, bg_r, we_r, be_r):
    d, tm = lay.d, lay.tm
    n_groups, per_group = we_r.shape[1], we_r.shape[2]
    n_exp = n_groups * per_group
    assert n_exp + n_groups <= LANES
    wr = jnp.concatenate([we_r.reshape(d, n_exp), wg_r], axis=1)
    wr = jnp.pad(wr, ((0, 0), (0, LANES - wr.shape[1])))
    rb = jnp.pad(jnp.concatenate([be_r.reshape(n_exp), bg_r]), (0, LANES - n_exp - n_groups)).reshape(1, LANES)
    w_hi = wr.astype(BF16)
    w_lo = (wr - w_hi.astype(F32)).astype(BF16)
    w2 = jnp.concatenate([w_hi, w_lo], axis=1)
    return pl.pallas_call(
        functools.partial(_normmod_router_kernel, n_groups=n_groups, per_group=per_group),
        grid=(lay.row_tiles,),
        in_specs=[pl.BlockSpec((tm, d), lambda i: (i, 0)),
                  pl.BlockSpec((1, d), lambda i: (0, 0))] + _mod_specs(lay, layer, shift_idx, scale_idx) + [
                  pl.BlockSpec((d, 2 * LANES), lambda i: (0, 0)),
                  pl.BlockSpec((d, LANES), lambda i: (0, 0)),
                  pl.BlockSpec((1, LANES), lambda i: (0, 0))],
        out_specs=[pl.BlockSpec((tm, d // 2), lambda i: (i, 0)),
                   pl.BlockSpec((tm, LANES), lambda i: (i, 0)),
                   pl.BlockSpec((tm, LANES), lambda i: (i, 0))],
        out_shape=[jax.ShapeDtypeStruct((lay.n_tok, d // 2), jnp.uint32),
                   jax.ShapeDtypeStruct((lay.n_tok, LANES), jnp.int32),
                   jax.ShapeDtypeStruct((lay.n_tok, LANES), F32)],
        compiler_params=_params(1),
        name="normmod_router",
    )(tok, g.reshape(1, d), mods, mods, w2, w_hi, rb)


def _final_norm_kernel(t_ref, g_ref, o_ref):
    o_ref[...] = _rms(t_ref[...], g_ref[...])


def _final_norm(lay, tok, g):
    d, tm = lay.d, lay.tm
    return pl.pallas_call(
        _final_norm_kernel,
        grid=(lay.lat_tiles,),
        in_specs=[pl.BlockSpec((tm, d), lambda i: (i, 0)),
                  pl.BlockSpec((1, d), lambda i: (0, 0))],
        out_specs=pl.BlockSpec((tm, d), lambda i: (i, 0)),
        out_shape=jax.ShapeDtypeStruct((lay.n_lat, d), F32),
        compiler_params=_params(1),
        name="final_norm",
    )(tok, g.reshape(1, d))


def _fused_matmul_kernel(*refs, n_a, n_w, n_extra, n_out, epilogue, lat_tiles):
    a_refs = refs[:n_a]
    w_refs = refs[n_a:n_a + n_w]
    e_refs = refs[n_a + n_w:n_a + n_w + n_extra]
    o_refs = refs[n_a + n_w + n_extra:n_a + n_w + n_extra + n_out]
    wb_refs = refs[n_a + n_w + n_extra + n_out:]

    @pl.when(pl.program_id(1) == 0)
    def _():
        for w_ref, wb_ref in zip(w_refs, wb_refs):
            wb_ref[...] = w_ref[...].astype(BF16)

    def run(a_ref):
        a = a_ref[...]
        accs = [jnp.dot(a, wb_ref[...], preferred_element_type=F32) for wb_ref in wb_refs]
        epilogue(accs, e_refs, o_refs)

    if n_a == 1:
        run(a_refs[0])
    else:
        is_lat = pl.program_id(1) < lat_tiles
        pl.when(is_lat)(lambda: run(a_refs[0]))
        pl.when(jnp.logical_not(is_lat))(lambda: run(a_refs[1]))


def _fused_matmul(a, w_specs, extras, outs, epilogue, *, tm, tn, n_col_blocks, name):
    if isinstance(a, tuple):
        a_lat, a_ctx = a
        k = a_lat.shape[1]
        lat_tiles = a_lat.shape[0] // tm
        assert a_ctx.shape == (tm, k)
        a_args = [a_lat, a_ctx]
        a_specs = [pl.BlockSpec((tm, k), lambda j, i: (jnp.minimum(i, lat_tiles - 1), 0)),
                   pl.BlockSpec((tm, k), lambda j, i: (0, 0))]
        row_tiles = lat_tiles + 1
    else:
        m, k = a.shape
        lat_tiles = None
        a_args = [a]
        a_specs = [pl.BlockSpec((tm, k), lambda j, i: (i, 0))]
        row_tiles = m // tm
    kernel = functools.partial(_fused_matmul_kernel, n_a=len(a_args), n_w=len(w_specs), n_extra=len(extras),
                               n_out=len(outs), epilogue=epilogue, lat_tiles=lat_tiles)
    res = pl.pallas_call(
        kernel,
        grid=(n_col_blocks, row_tiles),
        in_specs=a_specs + [s for _, s in w_specs] + [s for _, s in extras],
        out_specs=[s for _, s in outs],
        out_shape=[s for s, _ in outs],
        scratch_shapes=[pltpu.VMEM((k, tn), BF16) for _ in w_specs],
        compiler_params=_params(2),
        name=name,
    )(*a_args, *[w for w, _ in w_specs], *[e for e, _ in extras])
    return res


def _w2d(w, tn):
    if isinstance(w, tuple):
        stack, slot = w
        return (stack, pl.BlockSpec((None, stack.shape[1], tn), lambda j, i: (slot, 0, j)))
    return (w, pl.BlockSpec((w.shape[0], tn), lambda j, i: (0, j)))


def _tile_spec(tm, tn):
    return pl.BlockSpec((tm, tn), lambda j, i: (i, j))


def _row_spec(tn):
    return pl.BlockSpec((1, tn), lambda j, i: (0, j))


def _ep_store(dtype):
    def ep(accs, e_refs, o_refs):
        o_refs[0][...] = accs[0].astype(dtype)
    return ep


def _linear(a, w, *, tm, tn, dtype, name):
    n = w[0].shape[2] if isinstance(w, tuple) else w.shape[1]
    return _fused_matmul(a, [_w2d(w, tn)], [], [(jax.ShapeDtypeStruct((a.shape[0], n), dtype), _tile_spec(tm, tn))],
                         _ep_store(dtype), tm=tm, tn=tn, n_col_blocks=n // tn, name=name)[0]


def _ep_residual(accs, e_refs, o_refs):
    tok_ref, gate_ref = e_refs
    o_refs[0][...] = tok_ref[...] + gate_ref[...] * accs[0]


def _ep_residual_bias(accs, e_refs, o_refs):
    tok_ref, gate_ref, b_ref = e_refs
    o_refs[0][...] = tok_ref[...] + gate_ref[...] * (accs[0] + b_ref[...])


def _residual_linear(lay, a, w, tok, mods, layer, gate_idx, *, tn, bias=None, name):
    d, tm = lay.d, lay.tm
    cols = d // tn
    extras = [(tok, _tile_spec(tm, tn)),
              (mods, pl.BlockSpec((None, 1, tn), lambda j, i: (_mod_row(lay, layer, i), 0, gate_idx * cols + j)))]
    ep = _ep_residual
    if bias is not None:
        extras.append((bias.reshape(1, d), _row_spec(tn)))
        ep = _ep_residual_bias
    return _fused_matmul(a, [_w2d(w, tn)], extras, [(jax.ShapeDtypeStruct((lay.n_tok, d), F32), _tile_spec(tm, tn))],
                         ep, tm=tm, tn=tn, n_col_blocks=cols, name=name)[0]


def _flat_positions_table(lay, lat_table, ctx_row):
    lat = jnp.tile(lat_table, (lay.batch, 1))
    ctx = jnp.broadcast_to(ctx_row, (lay.batch * lay.ctx, lat_table.shape[1]))
    return jnp.concatenate([lat, ctx], axis=0)


def _mla_rope_tables(lay, rope_dim):
    rows = lay.seq // GRID_W
    grid = jnp.stack(jnp.meshgrid(jnp.arange(rows), jnp.arange(GRID_W), indexing='ij'), axis=-1)
    grid = grid.reshape(-1, 2).astype(F32)
    n_freq = rope_dim // 4
    inv = ROPE_BASE ** (-jnp.arange(n_freq, dtype=F32) / n_freq)
    ang = jnp.concatenate([grid[:, :1] * inv, grid[:, 1:] * inv], axis=-1)
    cos, sin = jnp.cos(ang), jnp.sin(ang)
    pad = LANES - rope_dim
    cos_t = jnp.concatenate([cos, cos, jnp.ones((lay.seq, pad), F32)], axis=-1)
    sin_t = jnp.concatenate([-sin, sin, jnp.zeros((lay.seq, pad), F32)], axis=-1)
    one = jnp.ones((1, LANES), F32)
    return _flat_positions_table(lay, cos_t, one), _flat_positions_table(lay, sin_t, 0.0 * one)


def _rope_slab(r, cos_t, sin_t, half):
    lane = lax.broadcasted_iota(jnp.int32, r.shape, 1)
    partner = jnp.where(lane < half, pltpu.roll(r, LANES - half, 1), pltpu.roll(r, half, 1))
    return r * cos_t + partner * sin_t


def _ret_tables(lay, dk):
    theta = 1.0 / (RET_THETA_BASE ** jnp.linspace(0.0, 1.0, dk // 2, dtype=F32))
    ang = jnp.arange(lay.seq, dtype=F32)[:, None] * theta
    one = jnp.ones((1, dk // 2), F32)
    return (_flat_positions_table(lay, jnp.cos(ang), one),
            _flat_positions_table(lay, jnp.sin(ang), 0.0 * one))


def _mla_norm_kernel(cq_ref, kv_ref, qn_ref, kvn_ref, cos_ref, sin_ref, cqn_ref, ckv_ref, kr_ref, *, kv_lora, rope):
    cqn_ref[...] = _rms(cq_ref[...], qn_ref[...]).astype(BF16)
    kv = kv_ref[...]
    ckv_ref[...] = _rms(kv[:, :kv_lora], kvn_ref[...]).astype(BF16)
    kr_ref[...] = _rope_slab(kv[:, kv_lora:], cos_ref[...], sin_ref[...], rope // 2).astype(BF16)


def _mla_norm(lay, cq, kv, q_norm, kv_norm, cos_t, sin_t, kv_lora, rope):
    tm = lay.tm
    q_lora = cq.shape[1]
    row = lambda w: pl.BlockSpec((tm, w), lambda i: (i, 0))
    const = lambda w: pl.BlockSpec((1, w), lambda i: (0, 0))
    return pl.pallas_call(
        functools.partial(_mla_norm_kernel, kv_lora=kv_lora, rope=rope),
        grid=(lay.row_tiles,),
        in_specs=[row(q_lora), row(kv_lora + LANES), const(q_lora), const(kv_lora), row(LANES), row(LANES)],
        out_specs=[row(q_lora), row(kv_lora), row(LANES)],
        out_shape=[jax.ShapeDtypeStruct((lay.n_tok, q_lora), BF16),
                   jax.ShapeDtypeStruct((lay.n_tok, kv_lora), BF16),
                   jax.ShapeDtypeStruct((lay.n_tok, LANES), BF16)],
        compiler_params=_params(1),
        name="mla_norm",
    )(cq, kv, q_norm.reshape(1, q_lora), kv_norm.reshape(1, kv_lora), cos_t, sin_t)


def _ep_mla_q(accs, e_refs, o_refs, *, heads_per_tile, rope):
    cos_ref, sin_ref = e_refs
    acc = accs[0]
    cos_t, sin_t = cos_ref[...], sin_ref[...]
    for h in range(heads_per_tile):
        base = h * 2 * LANES
        o_refs[0][:, base:base + LANES] = acc[:, base:base + LANES].astype(BF16)
        slab = _rope_slab(acc[:, base + LANES:base + 2 * LANES], cos_t, sin_t, rope // 2)
        o_refs[0][:, base + LANES:base + 2 * LANES] = slab.astype(BF16)


def _ep_mla_k(accs, e_refs, o_refs, *, heads_per_tile):
    kr = e_refs[0][...]
    acc = accs[0]
    for h in range(heads_per_tile):
        o_refs[0][:, h * 2 * LANES:h * 2 * LANES + LANES] = acc[:, h * LANES:(h + 1) * LANES].astype(BF16)
        o_refs[0][:, h * 2 * LANES + LANES:(h + 1) * 2 * LANES] = kr


def _attn_scores(q, kv_refs, chunks, s_ref):
    off = 0
    for ref_idx, start, size in chunks:
        k = kv_refs[2 * ref_idx][start:start + size, :]
        s_ref[:, off:off + size] = lax.dot_general(q, k, (((1,), (1,)), ((), ())), preferred_element_type=F32)
        off += size


ATTN_ROW_BLOCK = 64


def _attn_probs(s_ref, p_ref, l_ref, c):
    tq, n_keys = s_ref.shape
    rb = min(ATTN_ROW_BLOCK, tq)
    for r0 in range(0, tq, rb):
        m_acc = s_ref[r0:r0 + rb, 0:LANES]
        for j in range(1, n_keys // LANES):
            m_acc = jnp.maximum(m_acc, s_ref[r0:r0 + rb, j * LANES:(j + 1) * LANES])
        mc = jnp.broadcast_to(jnp.max(m_acc, axis=1, keepdims=True) * c, (rb, LANES))
        l_acc = jnp.zeros((rb, LANES), F32)
        for j in range(n_keys // LANES):
            p = jnp.exp2(s_ref[r0:r0 + rb, j * LANES:(j + 1) * LANES] * c - mc)
            l_acc = l_acc + p
            p_ref[r0:r0 + rb, j * LANES:(j + 1) * LANES] = p.astype(BF16)
        l_ref[r0:r0 + rb, :] = jnp.broadcast_to(jnp.sum(l_acc, axis=1, keepdims=True), (rb, LANES))


def _attn_output(p_ref, l_ref, kv_refs, n_kv):
    acc = None
    off = 0
    for ref_idx in range(n_kv):
        v_ref = kv_refs[2 * ref_idx + 1]
        n = v_ref.shape[0]
        part = jnp.dot(p_ref[:, off:off + n], v_ref[...], preferred_element_type=F32)
        acc = part if acc is None else acc + part
        off += n
    return acc / l_ref[...]


def _attn_kernel(q_ref, *refs, chunks, n_kv, scale, tq):
    kv_refs = refs[:2 * n_kv]
    o_ref = refs[2 * n_kv]
    scratch = refs[2 * n_kv + 1:]
    s_refs, p_refs, l_refs = scratch[0:2], scratch[2:4], scratch[4:6]
    assert o_ref.shape[1] == LANES
    nq = q_ref.shape[0] // tq
    c = scale * LOG2E

    def rows(i):
        return slice(i * tq, (i + 1) * tq)

    _attn_scores(q_ref[rows(0), :], kv_refs, chunks, s_refs[0])
    for i in range(nq):
        cur, nxt = i % 2, (i + 1) % 2
        if i >= 1:
            o_ref[rows(i - 1), :] = _attn_output(p_refs[nxt], l_refs[nxt], kv_refs, n_kv).astype(o_ref.dtype)
        if i + 1 < nq:
            _attn_scores(q_ref[rows(i + 1), :], kv_refs, chunks, s_refs[nxt])
        _attn_probs(s_refs[cur], p_refs[cur], l_refs[cur], c)
    last = (nq - 1) % 2
    o_ref[rows(nq - 1), :] = _attn_output(p_refs[last], l_refs[last], kv_refs, n_kv).astype(o_ref.dtype)


def _attn_scratch(tq, n_keys):
    return ([pltpu.VMEM((tq, n_keys), F32)] * 2 + [pltpu.VMEM((tq, n_keys), BF16)] * 2
            + [pltpu.VMEM((tq, LANES), F32)] * 2)


def _mla_attention(lay, q, k, v, heads, scale, tq, tk):
    b, s, c = lay.batch, lay.seq, lay.ctx
    qk_w, v_w = 2 * LANES, LANES
    ctx_blk0 = lay.n_lat // c
    lat_chunks = tuple([(0, 0, c)] + [(1, st, tk) for st in range(0, s, tk)])
    assert s % tq == 0 and s % tk == 0
    o_lat = pl.pallas_call(
        functools.partial(_attn_kernel, chunks=lat_chunks, n_kv=2, scale=scale, tq=tq),
        scratch_shapes=_attn_scratch(tq, c + s),
        grid=(b, heads),
        in_specs=[pl.BlockSpec((s, qk_w), lambda bi, h: (bi, h)),
                  pl.BlockSpec((c, qk_w), lambda bi, h: (ctx_blk0 + bi, h)),
                  pl.BlockSpec((c, v_w), lambda bi, h: (ctx_blk0 + bi, h)),
                  pl.BlockSpec((s, qk_w), lambda bi, h: (bi, h)),
                  pl.BlockSpec((s, v_w), lambda bi, h: (bi, h))],
        out_specs=pl.BlockSpec((s, v_w), lambda bi, h: (bi, h)),
        out_shape=jax.ShapeDtypeStruct((lay.n_lat, heads * v_w), BF16),
        compiler_params=_params(2),
        name="mla_attn_latent",
    )(q, k, v, k, v)
    o_ctx = pl.pallas_call(
        functools.partial(_attn_kernel, chunks=((0, 0, c),), n_kv=1, scale=scale, tq=c),
        scratch_shapes=_attn_scratch(c, c),
        grid=(b, heads),
        in_specs=[pl.BlockSpec((c, qk_w), lambda bi, h: (ctx_blk0 + bi, h)),
                  pl.BlockSpec((c, qk_w), lambda bi, h: (ctx_blk0 + bi, h)),
                  pl.BlockSpec((c, v_w), lambda bi, h: (ctx_blk0 + bi, h))],
        out_specs=pl.BlockSpec((c, v_w), lambda bi, h: (bi, h)),
        out_shape=jax.ShapeDtypeStruct((b * c, heads * v_w), BF16),
        compiler_params=_params(2),
        name="mla_attn_ctx",
    )(q, k, v)
    return o_lat, o_ctx


def _mla_mixer(lay, h, tok, mods, layer, slot, wq_a_all, q_norm, wq_b, wkv_a, kv_norm, wkv_b, wo_all):
    d, tm = lay.d, lay.tm
    q_lora = wq_a_all.shape[2]
    wq_a = (wq_a_all, slot)
    heads, qk = wq_b.shape[1], wq_b.shape[2]
    kv_lora = kv_norm.shape[0]
    rope = wkv_a.shape[1] - kv_lora
    nope = qk - rope
    v_dim = wkv_b.shape[2] - nope
    assert nope == LANES and v_dim == LANES and rope <= LANES and rope % 4 == 0
    cos_t, sin_t = _mla_rope_tables(lay, rope)

    cq = _linear(h, wq_a, tm=tm, tn=min(512, q_lora), dtype=F32, name="mla_q_a")
    wkv_a_pad = jnp.pad(wkv_a, ((0, 0), (0, LANES - rope)))
    kv = _linear(h, wkv_a_pad, tm=tm, tn=kv_lora + LANES, dtype=F32, name="mla_kv_a")
    cqn, ckv, kr = _mla_norm(lay, cq, kv, q_norm, kv_norm, cos_t, sin_t, kv_lora, rope)

    hpt = min(8, heads)
    wq = jnp.pad(wq_b, ((0, 0), (0, 0), (0, 2 * LANES - qk))).reshape(q_lora, heads * 2 * LANES)
    tn_q = hpt * 2 * LANES
    q = _fused_matmul(
        cqn, [_w2d(wq, tn_q)],
        [(cos_t, pl.BlockSpec((tm, LANES), lambda j, i: (i, 0))), (sin_t, pl.BlockSpec((tm, LANES), lambda j, i: (i, 0)))],
        [(jax.ShapeDtypeStruct((lay.n_tok, heads * 2 * LANES), BF16), _tile_spec(tm, tn_q))],
        functools.partial(_ep_mla_q, heads_per_tile=hpt, rope=rope),
        tm=tm, tn=tn_q, n_col_blocks=heads // hpt, name="mla_q_b")[0]

    wk = wkv_b[:, :, :nope].reshape(kv_lora, heads * nope)
    wv = wkv_b[:, :, nope:].reshape(kv_lora, heads * v_dim)
    hpt_k = min(16, heads)
    k = _fused_matmul(
        ckv, [_w2d(wk, hpt_k * LANES)],
        [(kr, pl.BlockSpec((tm, LANES), lambda j, i: (i, 0)))],
        [(jax.ShapeDtypeStruct((lay.n_tok, heads * 2 * LANES), BF16), _tile_spec(tm, hpt_k * 2 * LANES))],
        functools.partial(_ep_mla_k, heads_per_tile=hpt_k),
        tm=tm, tn=hpt_k * LANES, n_col_blocks=heads // hpt_k, name="mla_k_b")[0]
    v = _linear(ckv, wv, tm=tm, tn=min(2048, heads * v_dim), dtype=BF16, name="mla_v_b")

    o = _mla_attention(lay, q, k, v, heads, qk ** -0.5, tq=min(512, lay.seq // 2), tk=min(512, lay.seq))
    wo = (wo_all.reshape(wo_all.shape[0], heads * v_dim, d), slot)
    return _residual_linear(lay, o, wo, tok, mods, layer, 2, tn=min(512, d), name="mla_o")


def _ep_ret_qkv(accs, e_refs, o_refs, *, dk, k_scale):
    cos_ref, sin_ref = e_refs
    cos, sin = cos_ref[...], sin_ref[...]
    half = dk // 2
    for acc, o_ref, scale in ((accs[0], o_refs[0], 1.0), (accs[1], o_refs[1], k_scale)):
        x1, x2 = acc[:, :half], acc[:, half:]
        o_ref[:, :half] = ((x1 * cos - x2 * sin) * scale).astype(BF16)
        o_ref[:, half:] = ((x1 * sin + x2 * cos) * scale).astype(BF16)
    o_refs[2][...] = accs[2].astype(BF16)


def _ret_qkv(lay, a, wq, wk, wv, cos, sin, dk, k_scale):
    tm = lay.tm
    n = wq.shape[1]
    half = dk // 2
    out = (jax.ShapeDtypeStruct((lay.n_tok, n), BF16), _tile_spec(tm, dk))
    return _fused_matmul(
        a, [_w2d(wq, dk), _w2d(wk, dk), _w2d(wv, dk)],
        [(cos, pl.BlockSpec((tm, half), lambda j, i: (i, 0))), (sin, pl.BlockSpec((tm, half), lambda j, i: (i, 0)))],
        [out, out, out],
        functools.partial(_ep_ret_qkv, dk=dk, k_scale=k_scale),
        tm=tm, tn=dk, n_col_blocks=n // dk, name="ret_qkv")


def _retention_kernel(lg_ref, ql_ref, kl_ref, vl_ref, qc_ref, kc_ref, vc_ref, yl_ref, yc_ref,
                      sf_ref, sb_ref, yfl_ref, yfc_ref, ybl_ref, ybc_ref, *, chunk, heads):
    h = pl.program_id(1)
    lg_f = lg_ref[h]
    lg_b = lg_ref[heads + h]
    n_lat = ql_ref.shape[0] // chunk
    n_ctx = qc_ref.shape[0] // chunk
    dv = vl_ref.shape[1]

    row = lax.broadcasted_iota(jnp.int32, (chunk, chunk), 0).astype(F32)
    col = lax.broadcasted_iota(jnp.int32, (chunk, chunk), 1).astype(F32)
    pos = lax.broadcasted_iota(jnp.int32, (chunk, dv), 0).astype(F32)
    diff = row - col
    dmat_f = jnp.where(diff >= 0, jnp.exp(jnp.where(diff >= 0, diff, 0.0) * lg_f), 0.0)
    dmat_b = jnp.where(diff <= 0, jnp.exp(jnp.where(diff <= 0, -diff, 0.0) * lg_b), 0.0)
    qdec_f = jnp.exp((pos + 1.0) * lg_f)
    kdec_f = jnp.exp((chunk - 1.0 - pos) * lg_f)
    qdec_b = jnp.exp((chunk - pos) * lg_b)
    kdec_b = jnp.exp(pos * lg_b)
    blk_f = jnp.exp(chunk * lg_f)
    blk_b = jnp.exp(chunk * lg_b)

    def step(q, k, v, s_ref, dmat, qdec, kdec, blk):
        scores = lax.dot_general(q, k, (((1,), (1,)), ((), ())), preferred_element_type=F32) * dmat
        inner = jnp.dot(scores.astype(BF16), v, preferred_element_type=F32)
        state = s_ref[...]
        cross = jnp.dot(q, state.astype(BF16), preferred_element_type=F32) * qdec
        kd = (k.astype(F32) * kdec).astype(BF16)
        s_ref[...] = state * blk + lax.dot_general(kd, v, (((0,), (0,)), ((), ())), preferred_element_type=F32)
        return inner + cross

    def fwd(q_ref, k_ref, v_ref, y_ref, t):
        sl = pl.ds(pl.multiple_of(t * chunk, chunk), chunk)
        y_ref[sl, :] = step(q_ref[sl, :], k_ref[sl, :], v_ref[sl, :], sf_ref, dmat_f, qdec_f, kdec_f, blk_f)

    def bwd(q_ref, k_ref, v_ref, y_ref, t):
        sl = pl.ds(pl.multiple_of(t * chunk, chunk), chunk)
        y_ref[sl, :] = step(q_ref[sl, :], k_ref[sl, :], v_ref[sl, :], sb_ref, dmat_b, qdec_b, kdec_b, blk_b)

    sf_ref[...] = jnp.zeros_like(sf_ref)
    sb_ref[...] = jnp.zeros_like(sb_ref)
    for t in range(n_ctx):
        fwd(qc_ref, kc_ref, vc_ref, yfc_ref, t)
        bwd(qc_ref, kc_ref, vc_ref, ybc_ref, n_ctx - 1 - t)

    def body(t, carry):
        fwd(ql_ref, kl_ref, vl_ref, yfl_ref, t)
        bwd(ql_ref, kl_ref, vl_ref, ybl_ref, n_lat - 1 - t)
        return carry

    lax.fori_loop(0, n_lat, body, 0, unroll=4)

    def group_norm(y):
        mu = jnp.mean(y, axis=-1, keepdims=True)
        var = jnp.mean(jnp.square(y - mu), axis=-1, keepdims=True)
        return (y - mu) * lax.rsqrt(var + GROUP_NORM_EPS)

    for t in range(n_ctx):
        sl = pl.ds(t * chunk, chunk)
        yc_ref[sl, :] = group_norm(yfc_ref[sl, :] + ybc_ref[sl, :]).astype(BF16)

    def norm_body(t, carry):
        sl = pl.ds(pl.multiple_of(t * chunk, chunk), chunk)
        yl_ref[sl, :] = group_norm(yfl_ref[sl, :] + ybl_ref[sl, :]).astype(BF16)
        return carry

    lax.fori_loop(0, n_lat, norm_body, 0, unroll=4)


def _retention(lay, q, k, v, log_g, heads, dk, dv, chunk):
    b, s, c = lay.batch, lay.seq, lay.ctx
    ctx_blk0 = lay.n_lat // c
    lat = lambda w: pl.BlockSpec((s, w), lambda bi, h, lg: (bi, h))
    ctx = lambda w: pl.BlockSpec((c, w), lambda bi, h, lg: (ctx_blk0 + bi, h))
    y_lat, y_ctx = pl.pallas_call(
        functools.partial(_retention_kernel, chunk=chunk, heads=heads),
        grid_spec=pltpu.PrefetchScalarGridSpec(
            num_scalar_prefetch=1,
            grid=(b, heads),
            in_specs=[lat(dk), lat(dk), lat(dv), ctx(dk), ctx(dk), ctx(dv)],
            out_specs=[lat(dv), pl.BlockSpec((c, dv), lambda bi, h, lg: (bi, h))],
            scratch_shapes=[pltpu.VMEM((dk, dv), F32), pltpu.VMEM((dk, dv), F32),
                            pltpu.VMEM((s, dv), F32), pltpu.VMEM((c, dv), F32),
                            pltpu.VMEM((s, dv), F32), pltpu.VMEM((c, dv), F32)]),
        out_shape=[jax.ShapeDtypeStruct((lay.n_lat, heads * dv), BF16),
                   jax.ShapeDtypeStruct((b * c, heads * dv), BF16)],
        compiler_params=_params(2),
        name="retention",
    )(log_g.reshape(-1), q, k, v, q, k, v)
    return y_lat, y_ctx


def _ep_gate_mul(accs, e_refs, o_refs, *, lat_tiles):
    y_lat_ref, y_ctx_ref = e_refs
    y = jnp.where(pl.program_id(1) < lat_tiles, y_lat_ref[...], y_ctx_ref[...])
    o_refs[0][...] = (_silu(accs[0]) * y.astype(F32)).astype(BF16)


def _retention_mixer(lay, h, tok, mods, layer, wq, wk, wv, wg, wo, decay):
    d, tm = lay.d, lay.tm
    heads, dk = wq.shape[1], wq.shape[2]
    dv = wv.shape[2]
    assert dk == dv and dk % (2 * LANES) == 0
    chunk = min(256, lay.ctx)
    assert lay.ctx % chunk == 0 and lay.seq % chunk == 0
    log_g = -jnp.exp(decay.astype(F32))
    cos, sin = _ret_tables(lay, dk)
    q, k, v = _ret_qkv(lay, h, wq.reshape(d, heads * dk), wk.reshape(d, heads * dk), wv.reshape(d, heads * dv),
                       cos, sin, dk, dk ** -0.5)
    y_lat, y_ctx = _retention(lay, q, k, v, log_g, heads, dk, dv, chunk)
    tn = min(512, heads * dv)
    lat_tiles = lay.lat_tiles
    z = _fused_matmul(h, [_w2d(wg, tn)],
                      [(y_lat, pl.BlockSpec((tm, tn), lambda j, i: (jnp.minimum(i, lat_tiles - 1), j))),
                       (y_ctx, pl.BlockSpec((tm, tn), lambda j, i: (0, j)))],
                      [(jax.ShapeDtypeStruct((lay.n_tok, heads * dv), BF16), _tile_spec(tm, tn))],
                      functools.partial(_ep_gate_mul, lat_tiles=lat_tiles),
                      tm=tm, tn=tn, n_col_blocks=heads * dv // tn, name="ret_gate")[0]
    return _residual_linear(lay, z, wo, tok, mods, layer, 2, tn=min(512, d), name="ret_o")


def _ep_glu(accs, e_refs, o_refs):
    ba_ref, bg_ref = e_refs
    o_refs[0][...] = (accs[0] + ba_ref[...]) * jax.nn.sigmoid(accs[1] + bg_ref[...])


HALO = 16


SUBLANES = 8


def _dwconv_kernel(prev_ref, cur_ref, next_ref, w_ref, wb_ref, g_ref, b_ref, o_ref, buf_ref, acc_ref, *,
                   width, tt, tiles_per_seq, tiles_per_ctx, lat_tiles):
    i = pl.program_id(0)
    in_lat = i < lat_tiles
    pos = jnp.where(in_lat, i % tiles_per_seq, (i - lat_tiles) % tiles_per_ctx)
    n_seq_tiles = jnp.where(in_lat, tiles_per_seq, tiles_per_ctx)
    first = pos == 0
    last = pos == n_seq_tiles - 1
    buf_ref[0:HALO, :] = jnp.where(first, 0.0, prev_ref[...])
    buf_ref[HALO:HALO + tt, :] = cur_ref[...]
    buf_ref[HALO + tt:HALO + tt + HALO, :] = jnp.where(last, 0.0, next_ref[...])
    lead = HALO - width // 2
    n_out = tt // SUBLANES
    max_dblk = (lead + width - 1) // SUBLANES
    assert (n_out + max_dblk + 1) * SUBLANES <= tt + 2 * HALO
    sub = lax.broadcasted_iota(jnp.int32, (SUBLANES, LANES), 0)

    def lane_chunk(cidx, carry):
        ls = pl.ds(pl.multiple_of(cidx * LANES, LANES), LANES)
        bias = jnp.broadcast_to(wb_ref[:, ls], (SUBLANES, LANES))
        taps = [jnp.broadcast_to(w_ref[k:k + 1, ls], (SUBLANES, LANES)) for k in range(width)]

        def load(blk):
            v = buf_ref[blk * SUBLANES:(blk + 1) * SUBLANES, ls]
            return v, {s: pltpu.roll(v, SUBLANES - s, 0) for s in range(1, SUBLANES)}

        accs = {}
        nxt = load(0)
        for blk in range(n_out + max_dblk):
            (va, rolls_a), nxt = nxt, load(blk + 1)
            for s in range(SUBLANES):
                x = va if s == 0 else jnp.where(sub < SUBLANES - s, rolls_a[s], nxt[1][s])
                for dblk in range(max_dblk + 1):
                    k = SUBLANES * dblk + s - lead
                    out_blk = blk - dblk
                    if 0 <= k < width and 0 <= out_blk < n_out:
                        accs[out_blk] = accs.get(out_blk, bias) + x * taps[k]
            done = blk - max_dblk
            if done >= 0:
                acc_ref[done * SUBLANES:(done + 1) * SUBLANES, ls] = accs.pop(done)
        assert not accs
        return carry

    lax.fori_loop(0, cur_ref.shape[1] // LANES, lane_chunk, 0)

    ln_rows = 2 * SUBLANES

    def ln_block(rb, carry):
        rs = pl.ds(pl.multiple_of(rb * ln_rows, ln_rows), ln_rows)
        acc = acc_ref[rs, :]
        mu = jnp.mean(acc, axis=-1, keepdims=True)
        cen = acc - mu
        var = jnp.mean(cen * cen, axis=-1, keepdims=True)
        y = cen * lax.rsqrt(var + NORM_EPS) * g_ref[...] + b_ref[...]
        o_ref[rs, :] = _silu(y).astype(BF16)
        return carry

    lax.fori_loop(0, tt // ln_rows, ln_block, 0, unroll=4)


def _dwconv_ln_swish(lay, u, dw, dw_b, ln_g, ln_b):
    d = lay.d
    width = dw.shape[0]
    assert width // 2 <= HALO
    tt = min(256, lay.ctx)
    assert lay.ctx % tt == 0 and lay.seq % tt == 0 and tt % HALO == 0
    r = tt // HALO
    n_tiles = lay.n_tok // tt
    n_halo_blocks = lay.n_tok // HALO
    const = lambda rows: pl.BlockSpec((rows, d), lambda i: (0, 0))
    kern = functools.partial(_dwconv_kernel, width=width, tt=tt, tiles_per_seq=lay.seq // tt,
                             tiles_per_ctx=lay.ctx // tt, lat_tiles=lay.n_lat // tt)
    return pl.pallas_call(
        kern,
        grid=(n_tiles,),
        in_specs=[pl.BlockSpec((HALO, d), lambda i: (jnp.maximum(i * r - 1, 0), 0)),
                  pl.BlockSpec((tt, d), lambda i: (i, 0)),
                  pl.BlockSpec((HALO, d), lambda i: (jnp.minimum((i + 1) * r, n_halo_blocks - 1), 0)),
                  const(width), const(1), const(1), const(1)],
        out_specs=pl.BlockSpec((tt, d), lambda i: (i, 0)),
        out_shape=jax.ShapeDtypeStruct((lay.n_tok, d), BF16),
        scratch_shapes=[pltpu.VMEM((tt + 2 * HALO, d), F32), pltpu.VMEM((tt, d), F32)],
        compiler_params=_params(1),
        name="dwconv_ln_swish",
    )(u, u, u, dw, dw_b.reshape(1, d), ln_g.reshape(1, d), ln_b.reshape(1, d))


def _conformer_mixer(lay, h, tok, mods, layer, pw1, b1, dw, dw_b, ln_g, ln_b, pw2, b2):
    d, tm = lay.d, lay.tm
    tn = min(512, d)
    cols = d // tn
    b1r = b1.reshape(1, 2 * d)
    u = _fused_matmul(
        h,
        [(pw1, pl.BlockSpec((d, tn), lambda j, i: (0, j))), (pw1, pl.BlockSpec((d, tn), lambda j, i: (0, cols + j)))],
        [(b1r, pl.BlockSpec((1, tn), lambda j, i: (0, j))), (b1r, pl.BlockSpec((1, tn), lambda j, i: (0, cols + j)))],
        [(jax.ShapeDtypeStruct((lay.n_tok, d), F32), _tile_spec(tm, tn))],
        _ep_glu, tm=tm, tn=tn, n_col_blocks=cols, name="conv_pw1_glu")[0]
    a = _dwconv_ln_swish(lay, u, dw, dw_b, ln_g, ln_b)
    return _residual_linear(lay, a, pw2, tok, mods, layer, 2, tn=min(512, d), bias=b2, name="conv_pw2")


MOE_ROW_TILE = 256


def _dispatch_plan(idx, wts, n_exp, tmx):
    eid = idx[:, :MOE_TOP_K].reshape(-1)
    w = wts[:, :MOE_TOP_K].reshape(-1)
    n_pairs = eid.shape[0]
    onehot = (eid[:, None] == jnp.arange(n_exp, dtype=jnp.int32)[None, :]).astype(jnp.int32)
    csum = jnp.cumsum(onehot, axis=0)
    rank = jnp.sum((csum - onehot) * onehot, axis=1)
    counts = csum[-1]
    padded = ((counts + tmx - 1) // tmx) * tmx
    ends = jnp.cumsum(padded)
    starts = ends - padded
    pos = (starts[eid] + rank).astype(jnp.int32)
    n_tiles = (n_pairs + n_exp * (tmx - 1)) // tmx + 1
    n_slots = n_tiles * tmx
    n_used = (ends[-1] // tmx).astype(jnp.int32).reshape(1)
    tile_start = jnp.arange(n_tiles, dtype=jnp.int32) * tmx
    tile_expert = jnp.sum((tile_start[:, None] >= ends[None, :]).astype(jnp.int32), axis=1)
    tile_expert = jnp.minimum(tile_expert, n_exp - 1).astype(jnp.int32)
    slot_pair1 = jnp.zeros((n_slots,), jnp.int32).at[pos].set(jnp.arange(1, n_pairs + 1, dtype=jnp.int32),
                                                              unique_indices=True)
    slot_pair = jnp.maximum(slot_pair1 - 1, 0)
    src_token = slot_pair // MOE_TOP_K
    slot_w = jnp.where(slot_pair1 > 0, w[slot_pair], 0.0)
    pos_by_choice = pos.reshape(-1, MOE_TOP_K).T.reshape(-1)
    return pos_by_choice, src_token, slot_w.reshape(n_slots, 1), tile_expert, n_used


def _row_gather(src_hbm, dst, sem, index_of_row, n_rows, unroll=8):
    def body(r, carry):
        pltpu.make_async_copy(src_hbm.at[pl.ds(index_of_row(r), 1), :], dst.at[pl.ds(r, 1), :], sem).start()
        return carry
    lax.fori_loop(0, n_rows, body, 0, unroll=unroll)


def _row_gather_wait(src_hbm, dst, sem):
    pltpu.make_async_copy(src_hbm.at[pl.ds(0, dst.shape[0]), :], dst, sem).wait()


def _moe_up_kernel(te_ref, src_ref, nu_ref, hp_hbm, w1_ref, w3_ref, sw_ref, act_ref, buf, sem, w1b, w3b, *, tmx):
    t = pl.program_id(0)
    n_used = nu_ref[0]

    def start_tile(tile, slot, unroll):
        _row_gather(hp_hbm, buf.at[slot], sem.at[slot], lambda r: src_ref[tile * tmx + r], tmx, unroll)

    @pl.when(t == 0)
    def _():
        start_tile(0, 0, 8)

    @pl.when(t < n_used)
    def _():
        slot = t % 2
        new_expert = jnp.logical_or(t == 0, te_ref[t] != te_ref[jnp.maximum(t - 1, 0)])

        @pl.when(new_expert)
        def _():
            w1b[...] = w1_ref[...].astype(BF16)
            w3b[...] = w3_ref[...].astype(BF16)

        _row_gather_wait(hp_hbm, buf.at[slot], sem.at[slot])
        start_tile(jnp.minimum(t + 1, n_used - 1), 1 - slot, True)
        a_lo, a_hi = _unpack_bf16_pairs(buf[slot])
        half = a_lo.shape[1]
        u1 = (jnp.dot(a_lo, w1b[:half, :], preferred_element_type=F32)
              + jnp.dot(a_hi, w1b[half:, :], preferred_element_type=F32))
        u3 = (jnp.dot(a_lo, w3b[:half, :], preferred_element_type=F32)
              + jnp.dot(a_hi, w3b[half:, :], preferred_element_type=F32))
        act_ref[...] = (_silu(u1) * u3 * sw_ref[...]).astype(BF16)

    @pl.when(t == n_used)
    def _():
        _row_gather_wait(hp_hbm, buf.at[t % 2], sem.at[t % 2])

    @pl.when(t >= n_used)
    def _():
        act_ref[...] = jnp.zeros_like(act_ref)


def _moe_down_kernel(te_ref, nu_ref, act_ref, w2_ref, y_ref, w2b):
    t = pl.program_id(0)
    n_used = nu_ref[0]

    @pl.when(t < n_used)
    def _():
        new_expert = jnp.logical_or(t == 0, te_ref[t] != te_ref[jnp.maximum(t - 1, 0)])

        @pl.when(new_expert)
        def _():
            w2b[...] = w2_ref[...].astype(BF16)

        y_ref[...] = _pack_bf16_pairs(jnp.dot(act_ref[...], w2b[...], preferred_element_type=F32))

    @pl.when(t >= n_used)
    def _():
        y_ref[...] = jnp.zeros_like(y_ref)


def _moe_combine_kernel(pos_ref, y_hbm, tok_ref, gate_ref, *refs, tmc, n_steps, with_norm):
    if with_norm:
        g_ref, sh_ref, sc_ref, o_ref, h_ref, buf, sem = refs
    else:
        o_ref, buf, sem = refs
    i = pl.program_id(0)

    def start_tile(tile, slot):
        for choice in range(MOE_TOP_K):
            base = choice * n_steps * tmc + tile * tmc
            _row_gather(y_hbm, buf.at[slot, pl.ds(choice * tmc, tmc)], sem.at[slot],
                        lambda r, base=base: pos_ref[base + r], tmc)

    @pl.when(i == 0)
    def _():
        start_tile(0, 0)

    @pl.when(i + 1 < n_steps)
    def _():
        start_tile(i + 1, (i + 1) % 2)

    slot = i % 2
    _row_gather_wait(y_hbm, buf.at[slot], sem.at[slot])
    lo0, hi0 = _unpack_pairs_f32(buf[slot, 0:tmc, :])
    lo1, hi1 = _unpack_pairs_f32(buf[slot, tmc:2 * tmc, :])
    new = tok_ref[...] + gate_ref[...] * jnp.concatenate([lo0 + lo1, hi0 + hi1], axis=1)
    o_ref[...] = new
    if with_norm:
        h_ref[...] = (_rms(new, g_ref[...]) * (1.0 + sc_ref[...]) + sh_ref[...]).astype(BF16)


def _moe_routed(lay, hp, idx, wts, tok, mods, layer, w1, w3, w2, next_norm):
    d = lay.d
    n_exp, d_ff = w1.shape[1], w1.shape[3]
    tmx = MOE_ROW_TILE
    tmc = lay.tm // 2
    assert MOE_TOP_K == 2 and lay.n_tok % tmc == 0
    pos, src_token, slot_w, tile_expert, n_used = _dispatch_plan(idx, wts, n_exp, tmx)
    n_slots = src_token.shape[0]
    n_tiles = n_slots // tmx

    act = pl.pallas_call(
        functools.partial(_moe_up_kernel, tmx=tmx),
        grid_spec=pltpu.PrefetchScalarGridSpec(
            num_scalar_prefetch=3,
            grid=(n_tiles,),
            in_specs=[pl.BlockSpec(memory_space=pl.ANY),
                      pl.BlockSpec((None, None, d, d_ff), lambda t, te, src, nu: (layer, te[t], 0, 0)),
                      pl.BlockSpec((None, None, d, d_ff), lambda t, te, src, nu: (layer, te[t], 0, 0)),
                      pl.BlockSpec((tmx, 1), lambda t, te, src, nu: (t, 0))],
            out_specs=pl.BlockSpec((tmx, d_ff), lambda t, te, src, nu: (t, 0)),
            scratch_shapes=[pltpu.VMEM((2, tmx, d // 2), jnp.uint32), pltpu.SemaphoreType.DMA((2,)),
                            pltpu.VMEM((d, d_ff), BF16), pltpu.VMEM((d, d_ff), BF16)]),
        out_shape=jax.ShapeDtypeStruct((n_slots, d_ff), BF16),
        compiler_params=_params(1),
        name="moe_up",
    )(tile_expert, src_token, n_used, hp, w1, w3, slot_w)

    y = pl.pallas_call(
        _moe_down_kernel,
        grid_spec=pltpu.PrefetchScalarGridSpec(
            num_scalar_prefetch=2,
            grid=(n_tiles,),
            in_specs=[pl.BlockSpec((tmx, d_ff), lambda t, te, nu: (t, 0)),
                      pl.BlockSpec((None, None, d_ff, d), lambda t, te, nu: (layer, te[t], 0, 0))],
            out_specs=pl.BlockSpec((tmx, d // 2), lambda t, te, nu: (t, 0)),
            scratch_shapes=[pltpu.VMEM((d_ff, d), BF16)]),
        out_shape=jax.ShapeDtypeStruct((n_slots, d // 2), jnp.uint32),
        compiler_params=_params(1),
        name="moe_down",
    )(tile_expert, n_used, act, w2)

    n_steps = lay.n_tok // tmc
    gate_rows = lay.tm // tmc
    with_norm = next_norm is not None

    def mod_spec(mod_layer, which):
        return pl.BlockSpec((None, 1, d), lambda i, pos: (_mod_row(lay, mod_layer, i // gate_rows), 0, which))

    tile = pl.BlockSpec((tmc, d), lambda i, pos: (i, 0))
    in_specs = [pl.BlockSpec(memory_space=pl.ANY), tile, mod_spec(layer, 5)]
    args = [pos, y, tok, mods]
    out_specs, out_shape = [tile], [jax.ShapeDtypeStruct((lay.n_tok, d), F32)]
    if with_norm:
        in_specs += [pl.BlockSpec((1, d), lambda i, pos: (0, 0)), mod_spec(layer + 1, 0), mod_spec(layer + 1, 1)]
        args += [next_norm.reshape(1, d), mods, mods]
        out_specs.append(tile)
        out_shape.append(jax.ShapeDtypeStruct((lay.n_tok, d), BF16))
    res = pl.pallas_call(
        functools.partial(_moe_combine_kernel, tmc=tmc, n_steps=n_steps, with_norm=with_norm),
        grid_spec=pltpu.PrefetchScalarGridSpec(
            num_scalar_prefetch=1,
            grid=(n_steps,),
            in_specs=in_specs,
            out_specs=out_specs,
            scratch_shapes=[pltpu.VMEM((2, MOE_TOP_K * tmc, d // 2), jnp.uint32), pltpu.SemaphoreType.DMA((2,))]),
        out_shape=out_shape,
        compiler_params=_params(1),
        name="moe_combine",
    )(*args)
    return (res[0], res[1]) if with_norm else (res[0], None)


def kernel(x, c, ctx, c_ctx, ada_w, ada_b, norm_mix, norm_ffn, mla_wq_a, mla_q_norm, mla_wq_b, mla_wkv_a, mla_kv_norm, mla_wkv_b, mla_wo, ret_wq, ret_wk, ret_wv, ret_wg, ret_wo, ret_decay, conv_pw1, conv_b1, conv_dw, conv_dw_b, conv_ln_g, conv_ln_b, conv_pw2, conv_b2, moe_wg_router, moe_bg_router, moe_we_router, moe_be_router, moe_w1, moe_w3, moe_w2, final_norm):
    b, s, d = x.shape
    n_ctx = ctx.shape[1]
    depth = ada_w.shape[0]
    n_mixers = 3
    lay = Layout(batch=b, seq=s, ctx=n_ctx, d=d, tm=b * n_ctx)
    assert s % lay.tm == 0 and b + 1 <= MOD_ROWS and s % GRID_W == 0

    tok = jnp.concatenate([x.reshape(b * s, d), ctx.reshape(b * n_ctx, d)], axis=0)
    cond = jnp.zeros((MOD_ROWS, d), F32).at[:b].set(c).at[b].set(c_ctx)
    mods = _adaln(cond, ada_w, ada_b, tk=min(LANES, d)).reshape(depth * MOD_ROWS, 1, N_MOD * d)

    h = _normmod(lay, tok, norm_mix[0], mods, 0, 0, 1)
    for i in range(depth):
        kind, slot = i % n_mixers, i // n_mixers
        if kind == 0:
            tok = _mla_mixer(lay, h, tok, mods, i, slot, mla_wq_a, mla_q_norm[slot], mla_wq_b[slot], mla_wkv_a[slot],
                             mla_kv_norm[slot], mla_wkv_b[slot], mla_wo)
        elif kind == 1:
            tok = _retention_mixer(lay, h, tok, mods, i, ret_wq[slot], ret_wk[slot], ret_wv[slot], ret_wg[slot],
                                   ret_wo[slot], ret_decay[slot])
        else:
            tok = _conformer_mixer(lay, h, tok, mods, i, conv_pw1[slot], conv_b1[slot], conv_dw[slot], conv_dw_b[slot],
                                   conv_ln_g[slot], conv_ln_b[slot], conv_pw2[slot], conv_b2[slot])
        hp, idx, wts = _normmod_router(lay, tok, norm_ffn[i], mods, i, 3, 4, moe_wg_router[i], moe_bg_router[i],
                                       moe_we_router[i], moe_be_router[i])
        next_norm = norm_mix[i + 1] if i + 1 < depth else None
        tok, h = _moe_routed(lay, hp, idx, wts, tok, mods, i, moe_w1, moe_w3, moe_w2, next_norm)
    return _final_norm(lay, tok, final_norm).reshape(b, s, d)
```
